```python
import jax
import jax.numpy as jnp
from jax import lax
import numpy as np

D_MODEL = 2048
BATCH = 4
SEQ = 2048
DEPTH = 2
DEC_BATCH = 128
DEC_SEQ = 1
PAST_LEN = 2048
PAGE_SIZE = 128

W_A = 2048
N_A_GROUPS = 8
A_GROUP = W_A // N_A_GROUPS
CHUNK = 128
W_B = 2048
N_B_BLOCKS = 16
B_BLOCK = W_B // N_B_BLOCKS
CONV_W = 4
C_RG = 8.0
N_HEADS = 16
HEAD_DIM = 128
N_KV = 4
N_IDX_HEADS = 16
IDX_DIM = 64
TOPK_MAX = 256
QBLOCK = 128
N_BRANCH = 3
EPS = 1e-6
SPLIT_SIZES = (W_A, W_A, W_A, W_B, W_B, N_HEADS * HEAD_DIM, N_KV * HEAD_DIM, N_KV * HEAD_DIM, N_IDX_HEADS * IDX_DIM, IDX_DIM, N_IDX_HEADS, N_HEADS * HEAD_DIM, N_BRANCH * D_MODEL)
N_IN = sum(SPLIT_SIZES)

kernel_name = 'hybrid_gmlp_rglru_dsa_decode_step'


def rmsnorm(x, g):
    x32 = x.astype(jnp.float32)
    y = x32 * lax.rsqrt(jnp.mean(x32 * x32, axis=-1, keepdims=True) + EPS)
    return (y * g.astype(jnp.float32)).astype(x.dtype)


def split_proj(p):
    points = [int(o) for o in np.cumsum(SPLIT_SIZES)[:-1]]
    return jnp.split(p, points, axis=-1)


def gmlp_spatial(u, v, w_s, b_s):
    B, T, _ = v.shape
    nc = -(-T // CHUNK)
    vp = jnp.pad(v, ((0, 0), (0, nc * CHUNK - T), (0, 0)))
    vc = vp.reshape(B, nc, CHUNK, N_A_GROUPS, A_GROUP)
    ws = w_s * jnp.tril(jnp.ones((CHUNK, CHUNK), w_s.dtype))
    s = jnp.einsum('gij,bcjgd->bcigd', ws, vc) + jnp.swapaxes(b_s, 0, 1)[None, None, :, :, None]
    return u * s.reshape(B, nc * CHUNK, W_A)[:, :T]


def causal_conv(x, prefix, w, b):
    T = x.shape[1]
    xp = jnp.concatenate([prefix.astype(x.dtype), x], axis=1)
    y = b
    for j in range(CONV_W):
        y = y + xp[:, j:j + T] * w[j]
    return y, xp[:, T:]


def rg_lru(xc, h0, w_a, b_a, w_x, b_x, lam):
    B, T, _ = xc.shape
    xb = xc.reshape(B, T, N_B_BLOCKS, B_BLOCK)
    r = jax.nn.sigmoid((jnp.einsum('btnc,ncd->btnd', xb, w_a).reshape(B, T, W_B) + b_a).astype(jnp.float32))
    i = jax.nn.sigmoid((jnp.einsum('btnc,ncd->btnd', xb, w_x).reshape(B, T, W_B) + b_x).astype(jnp.float32))
    log_a = C_RG * r * jax.nn.log_sigmoid(lam.astype(jnp.float32))
    a = jnp.exp(log_a)
    bt = jnp.sqrt(-jnp.expm1(2.0 * log_a)) * i * xc.astype(jnp.float32)

    def step(h, ab):
        a_t, b_t = ab
        h = a_t * h + b_t
        return h, h

    hT, hs = lax.scan(step, h0.astype(jnp.float32), (jnp.swapaxes(a, 0, 1), jnp.swapaxes(bt, 0, 1)))
    return jnp.swapaxes(hs, 0, 1).astype(xc.dtype), hT.astype(xc.dtype)


def dsa_core(q, q_idx, w_idx, k_idx_keys, q_pos, top_k, gather_kv):
    B, T = q.shape[:2]
    logits = jnp.einsum('bthd,bsd->bths', q_idx, k_idx_keys).astype(jnp.float32) * IDX_DIM ** -0.5
    score = jnp.einsum('bths,bth->bts', jax.nn.relu(logits), w_idx.astype(jnp.float32)) * N_IDX_HEADS ** -0.5
    key_pos = jnp.arange(k_idx_keys.shape[1], dtype=jnp.int32)
    score = jnp.where((key_pos[None, :] <= q_pos[:, None])[None], score, -jnp.inf)
    _, idx = lax.top_k(score, top_k)
    valid = idx <= q_pos[None, :, None]
    k_sel, v_sel = gather_kv(idx)
    qg = q.reshape(B, T, N_KV, N_HEADS // N_KV, HEAD_DIM)
    s = jnp.einsum('btngd,btsnd->btngs', qg, k_sel).astype(jnp.float32) * HEAD_DIM ** -0.5
    s = jnp.where(valid[:, :, None, None, :], s, -jnp.inf)
    p = jax.nn.softmax(s, axis=-1).astype(v_sel.dtype)
    o = jnp.einsum('btngs,btsnd->btngd', p, v_sel)
    return o.reshape(B, T, N_HEADS * HEAD_DIM)


def take_rows(a, idx):
    return jax.vmap(lambda ab, ib: ab[ib])(a, idx)


def attend_prompt(q, k, v, q_idx, k_idx, w_idx):
    B, T = q.shape[:2]
    nb = T // QBLOCK
    top_k = min(TOPK_MAX, T // 4)

    def gather_kv(idx):
        return take_rows(k, idx), take_rows(v, idx)

    def blocks(a):
        return jnp.swapaxes(a.reshape((B, nb, QBLOCK) + a.shape[2:]), 0, 1)

    pos = jnp.arange(T, dtype=jnp.int32).reshape(nb, QBLOCK)

    def one(args):
        qb, qib, wib, pb = args
        return dsa_core(qb, qib, wib, k_idx, pb, top_k, gather_kv)

    out = lax.map(one, (blocks(q), blocks(q_idx), blocks(w_idx), pos))
    return jnp.swapaxes(out, 0, 1).reshape(B, T, N_HEADS * HEAD_DIM)


def make_attend_sample(ck, cv, ckidx, page_table):
    def attend(q, k, v, q_idx, k_idx, w_idx):
        Bd, T = q.shape[:2]
        top_k = min(TOPK_MAX, (PAST_LEN + T) // 4)
        kidx_past = ckidx[page_table].reshape(Bd, PAST_LEN, IDX_DIM)
        keys = jnp.concatenate([kidx_past, k_idx.astype(kidx_past.dtype)], axis=1)
        pool_k = ck.reshape(-1, N_KV, HEAD_DIM)
        pool_v = cv.reshape(-1, N_KV, HEAD_DIM)

        def gather_kv(idx):
            is_past = (idx < PAST_LEN)[..., None, None]
            pc = jnp.minimum(idx, PAST_LEN - 1)
            phys = jnp.take_along_axis(page_table, (pc // PAGE_SIZE).reshape(Bd, -1), axis=1).reshape(idx.shape)
            row = phys * PAGE_SIZE + pc % PAGE_SIZE
            ni = jnp.clip(idx - PAST_LEN, 0, T - 1)
            k_sel = jnp.where(is_past, pool_k[row], take_rows(k, ni).astype(pool_k.dtype))
            v_sel = jnp.where(is_past, pool_v[row], take_rows(v, ni).astype(pool_v.dtype))
            return k_sel, v_sel

        q_pos = PAST_LEN + jnp.arange(T, dtype=jnp.int32)
        return dsa_core(q, q_idx, w_idx, keys, q_pos, top_k, gather_kv)

    return attend


def trunk_layer(x, c, prm, conv_prefix, h0, attend):
    B, T, _ = x.shape
    mod = jax.nn.silu(c) @ prm['w_mod'] + prm['b_mod']
    shift, scale, gate = jnp.split(mod[:, None, :], 3, axis=-1)
    h = rmsnorm(x, prm['g_pre']) * (1 + scale) + shift
    (u_a, v_a, z_a, x_b, z_b, q, k, v, q_i, k_i, w_i, z_c, g_m) = split_proj(h @ prm['w_in'])
    u_a = jax.nn.gelu(u_a)
    v_a = rmsnorm(jax.nn.gelu(v_a), prm['g_v'])
    o_a = gmlp_spatial(u_a, v_a, prm['w_s'], prm['b_s'])
    x_c, new_conv = causal_conv(x_b, conv_prefix, prm['w_conv'], prm['b_conv'])
    o_b, h_t = rg_lru(x_c, h0, prm['w_rg_a'], prm['b_rg_a'], prm['w_rg_x'], prm['b_rg_x'], prm['lam'])
    q = q.reshape(B, T, N_HEADS, HEAD_DIM)
    k = k.reshape(B, T, N_KV, HEAD_DIM)
    v = v.reshape(B, T, N_KV, HEAD_DIM)
    q_i = q_i.reshape(B, T, N_IDX_HEADS, IDX_DIM)
    o_c = attend(q, k, v, q_i, k_i, w_i)
    p_a = (o_a * jax.nn.silu(z_a)) @ prm['w_pa']
    p_b = (o_b * jax.nn.silu(z_b)) @ prm['w_pb']
    p_c = (o_c.astype(x.dtype) * jax.nn.silu(z_c)) @ prm['w_pc']
    g = jax.nn.sigmoid(g_m).reshape(B, T, N_BRANCH, D_MODEL)
    merged = g[:, :, 0] * p_a + g[:, :, 1] * p_b + g[:, :, 2] * p_c
    o = merged @ prm['w_out']
    x = x + gate * rmsnorm(o, prm['g_post'])
    return x, (k, v, k_i, new_conv, h_t, v_a)


def setup_inputs(seed: int = 0) -> dict:
    kit = iter(jax.random.split(jax.random.key(seed), 40))
    f32 = jnp.float32

    def nrm(shape, s=1.0):
        return jax.random.normal(next(kit), shape, f32) * s

    n_pages = PAST_LEN // PAGE_SIZE
    n_phys = (DEC_BATCH * n_pages * 5) // 4
    a_c = jax.random.uniform(next(kit), (DEPTH, W_B), f32, 0.9, 0.999)
    p = a_c ** (1.0 / C_RG)
    perm = jax.random.permutation(next(kit), n_phys)
    return {
        'x_prompt': nrm((BATCH, SEQ, D_MODEL)),
        'x_sample': nrm((DEC_BATCH, DEC_SEQ, D_MODEL)),
        'c_prompt': nrm((BATCH, D_MODEL)),
        'c_sample': nrm((DEC_BATCH, D_MODEL)),
        'cache_k': nrm((DEPTH, n_phys, PAGE_SIZE, N_KV, HEAD_DIM)),
        'cache_v': nrm((DEPTH, n_phys, PAGE_SIZE, N_KV, HEAD_DIM)),
        'cache_kidx': nrm((DEPTH, n_phys, PAGE_SIZE, IDX_DIM)),
        'state_conv': nrm((DEPTH, DEC_BATCH, CONV_W - 1, W_B)),
        'state_h': nrm((DEPTH, DEC_BATCH, W_B), 0.5),
        'page_table': perm[:DEC_BATCH * n_pages].reshape(DEC_BATCH, n_pages).astype(jnp.int32),
        'w_mod': nrm((DEPTH, D_MODEL, 3 * D_MODEL), 0.5 * D_MODEL ** -0.5),
        'b_mod': nrm((DEPTH, 3 * D_MODEL), 0.01),
        'g_pre': 1.0 + nrm((DEPTH, D_MODEL), 0.02),
        'w_in': nrm((DEPTH, D_MODEL, N_IN), D_MODEL ** -0.5),
        'g_v': 1.0 + nrm((DEPTH, W_A), 0.02),
        'w_s': nrm((DEPTH, N_A_GROUPS, CHUNK, CHUNK), CHUNK ** -0.5),
        'b_s': nrm((DEPTH, N_A_GROUPS, CHUNK), 0.01),
        'w_conv': nrm((DEPTH, CONV_W, W_B), CONV_W ** -0.5),
        'b_conv': nrm((DEPTH, W_B), 0.01),
        'w_rg_a': nrm((DEPTH, N_B_BLOCKS, B_BLOCK, B_BLOCK), B_BLOCK ** -0.5),
        'b_rg_a': nrm((DEPTH, W_B), 0.01),
        'w_rg_x': nrm((DEPTH, N_B_BLOCKS, B_BLOCK, B_BLOCK), B_BLOCK ** -0.5),
        'b_rg_x': nrm((DEPTH, W_B), 0.01),
        'lam': jnp.log(p) - jnp.log1p(-p),
        'w_pa': nrm((DEPTH, W_A, D_MODEL), W_A ** -0.5),
        'w_pb': nrm((DEPTH, W_B, D_MODEL), W_B ** -0.5),
        'w_pc': nrm((DEPTH, N_HEADS * HEAD_DIM, D_MODEL), (N_HEADS * HEAD_DIM) ** -0.5),
        'w_out': nrm((DEPTH, D_MODEL, D_MODEL), D_MODEL ** -0.5),
        'g_post': 1.0 + nrm((DEPTH, D_MODEL), 0.02),
    }


def reference(x_prompt, x_sample, c_prompt, c_sample, cache_k, cache_v, cache_kidx, state_conv, state_h, page_table, w_mod, b_mod, g_pre, w_in, g_v, w_s, b_s, w_conv, b_conv, w_rg_a, b_rg_a, w_rg_x, b_rg_x, lam, w_pa, w_pb, w_pc, w_out, g_post):
    y_p, y_s = x_prompt, x_sample
    bp = x_prompt.shape[0]
    kp, vp, kip, cvp, hp = [], [], [], [], []
    ks, vs, kis, cvs, hs, gvs = [], [], [], [], [], []
    for l in range(DEPTH):
        prm = {'w_mod': w_mod[l], 'b_mod': b_mod[l], 'g_pre': g_pre[l], 'w_in': w_in[l], 'g_v': g_v[l],
               'w_s': w_s[l], 'b_s': b_s[l], 'w_conv': w_conv[l], 'b_conv': b_conv[l],
               'w_rg_a': w_rg_a[l], 'b_rg_a': b_rg_a[l], 'w_rg_x': w_rg_x[l], 'b_rg_x': b_rg_x[l], 'lam': lam[l],
               'w_pa': w_pa[l], 'w_pb': w_pb[l], 'w_pc': w_pc[l], 'w_out': w_out[l], 'g_post': g_post[l]}
        zero_conv = jnp.zeros((bp, CONV_W - 1, W_B), x_prompt.dtype)
        zero_h = jnp.zeros((bp, W_B), x_prompt.dtype)
        y_p, st_p = trunk_layer(y_p, c_prompt, prm, zero_conv, zero_h, attend_prompt)
        attend_s = make_attend_sample(cache_k[l], cache_v[l], cache_kidx[l], page_table)
        y_s, st_s = trunk_layer(y_s, c_sample, prm, state_conv[l], state_h[l], attend_s)
        kp.append(st_p[0]); vp.append(st_p[1]); kip.append(st_p[2]); cvp.append(st_p[3]); hp.append(st_p[4])
        ks.append(st_s[0]); vs.append(st_s[1]); kis.append(st_s[2]); cvs.append(st_s[3]); hs.append(st_s[4]); gvs.append(st_s[5])
    return (y_p, y_s, jnp.stack(kp), jnp.stack(vp), jnp.stack(kip), jnp.stack(cvp), jnp.stack(hp), jnp.stack(ks), jnp.stack(vs), jnp.stack(kis), jnp.stack(cvs), jnp.stack(hs), jnp.stack(gvs))
```

```python
import functools

import jax
import jax.numpy as jnp
from jax import lax
from jax.experimental import pallas as pl
from jax.experimental.pallas import tpu as pltpu

F32 = jnp.float32
BF16 = jnp.bfloat16
INT_MIN = -(2 ** 31)

N_A_GROUPS = 8
CHUNK = 128
N_B_BLOCKS = 16
CONV_W = 4
C_RG = 8.0
N_HEADS = 16
HEAD_DIM = 128
N_KV = 4
N_IDX_HEADS = 16
IDX_DIM = 64
TOPK_MAX = 256
QBLOCK = 128
PAGE_SIZE = 128
N_BRANCH = 3
EPS = 1e-6

V7X_VMEM_LIMIT_BYTES = 56 * 1024 * 1024
SUBLANES = 8
LANES = 128


def _cparams(*sem):
    return pltpu.CompilerParams(dimension_semantics=sem, vmem_limit_bytes=V7X_VMEM_LIMIT_BYTES)


def _silu(x):
    return x * jax.nn.sigmoid(x)


def _apply_act(act, x):
    if act == "gelu":
        return jax.nn.gelu(x)
    if act == "silu":
        return _silu(x)
    if act == "sigmoid":
        return jax.nn.sigmoid(x)
    assert act == "none"
    return x


def _mm_body(x_ref, w_ref, o_ref, *, act, scale):
    acc = jnp.dot(x_ref[...], w_ref[...], preferred_element_type=F32)
    if scale != 1.0:
        acc = acc * scale
    o_ref[...] = _apply_act(act, acc).astype(o_ref.dtype)


def _matmul(x, w, *, out_dtype, act="none", scale=1.0, tm, tn):
    m, k = x.shape
    n = w.shape[1]
    tm, tn = min(tm, m), min(tn, n)
    assert m % tm == 0 and n % tn == 0
    return pl.pallas_call(
        functools.partial(_mm_body, act=act, scale=scale),
        out_shape=jax.ShapeDtypeStruct((m, n), out_dtype),
        grid=(m // tm, n // tn),
        in_specs=[pl.BlockSpec((tm, k), lambda i, j: (i, 0)),
                  pl.BlockSpec((k, tn), lambda i, j: (0, j))],
        out_specs=pl.BlockSpec((tm, tn), lambda i, j: (i, j)),
        compiler_params=_cparams("parallel", "arbitrary"),
        name="proj",
    )(x, w)


def _mod_body(c_ref, w_ref, b_ref, o_ref):
    s = _silu(c_ref[...]).astype(BF16)
    o_ref[...] = jnp.dot(s, w_ref[...], preferred_element_type=F32) + b_ref[...]


def _modulation(c, w, b, *, tn=1024):
    m, k = c.shape
    n = w.shape[1]
    return pl.pallas_call(
        _mod_body,
        out_shape=jax.ShapeDtypeStruct((m, n), F32),
        grid=(n // tn,),
        in_specs=[pl.BlockSpec((m, k), lambda j: (0, 0)),
                  pl.BlockSpec((k, tn), lambda j: (0, j)),
                  pl.BlockSpec((1, tn), lambda j: (0, j))],
        out_specs=pl.BlockSpec((m, tn), lambda j: (0, j)),
        compiler_params=_cparams("arbitrary"),
        name="modulation",
    )(c, w, b)


def _mod_spec(per_row, which, tm, d, rows_per_seq):
    if per_row:
        return pl.BlockSpec((tm, d), lambda i, *_: (i, which))
    return pl.BlockSpec((None, 1, d), lambda i, *_: ((i * tm) // rows_per_seq, 0, which))


def _prenorm_body(x_ref, g_ref, sh_ref, sc_ref, o_ref):
    x = x_ref[...]
    y = x * lax.rsqrt(jnp.mean(x * x, axis=-1, keepdims=True) + EPS) * g_ref[...]
    o_ref[...] = (y * (1.0 + sc_ref[...]) + sh_ref[...]).astype(o_ref.dtype)


def _prenorm(x, g, mod, *, per_row, rows_per_seq, tm):
    m, d = x.shape
    tm = min(tm, m)
    return pl.pallas_call(
        _prenorm_body,
        out_shape=jax.ShapeDtypeStruct((m, d), BF16),
        grid=(m // tm,),
        in_specs=[pl.BlockSpec((tm, d), lambda i: (i, 0)),
                  pl.BlockSpec((1, d), lambda i: (0, 0)),
                  _mod_spec(per_row, 0, tm, d, rows_per_seq),
                  _mod_spec(per_row, 1, tm, d, rows_per_seq)],
        out_specs=pl.BlockSpec((tm, d), lambda i: (i, 0)),
        compiler_params=_cparams("parallel"),
        name="prenorm",
    )(x, g, mod, mod)


def _gmlp_prompt_body(u_ref, vg_ref, za_ref, gv_ref, ws_ref, bs_ref, o_ref):
    v = vg_ref[...]
    vn = v * lax.rsqrt(jnp.mean(v * v, axis=-1, keepdims=True) + EPS) * gv_ref[...]
    vnb = vn.astype(BF16)
    w_a = vn.shape[1]
    gw = w_a // N_A_GROUPS
    row = lax.broadcasted_iota(jnp.int32, (CHUNK, CHUNK), 0)
    col = lax.broadcasted_iota(jnp.int32, (CHUNK, CHUNK), 1)
    causal = col <= row
    for g in range(N_A_GROUPS):
        ws = jnp.where(causal, ws_ref[g], 0.0).astype(BF16)
        sl = slice(g * gw, (g + 1) * gw)
        s = jnp.dot(ws, vnb[:, sl], preferred_element_type=F32) + bs_ref[:, g:g + 1]
        o = u_ref[:, sl].astype(F32) * s * za_ref[:, sl].astype(F32)
        o_ref[:, sl] = o.astype(o_ref.dtype)


def _gmlp_prompt(u, vg, za, g_v, w_s, b_s_t):
    m, w_a = u.shape
    row_spec = pl.BlockSpec((CHUNK, w_a), lambda i: (i, 0))
    return pl.pallas_call(
        _gmlp_prompt_body,
        out_shape=jax.ShapeDtypeStruct((m, w_a), BF16),
        grid=(m // CHUNK,),
        in_specs=[row_spec, row_spec, row_spec,
                  pl.BlockSpec((1, w_a), lambda i: (0, 0)),
                  pl.BlockSpec(w_s.shape, lambda i: (0, 0, 0)),
                  pl.BlockSpec(b_s_t.shape, lambda i: (0, 0))],
        out_specs=row_spec,
        compiler_params=_cparams("parallel"),
        name="gmlp_prompt",
    )(u, vg, za, g_v, w_s, b_s_t)


def _gmlp_sample_body(u_ref, vg_ref, za_ref, gv_ref, w0_ref, b0_ref, o_ref, vn_ref):
    v = vg_ref[...]
    vn = v * lax.rsqrt(jnp.mean(v * v, axis=-1, keepdims=True) + EPS) * gv_ref[...]
    vn_ref[...] = vn
    s = vn * w0_ref[...] + b0_ref[...]
    o_ref[...] = (u_ref[...].astype(F32) * s * za_ref[...].astype(F32)).astype(o_ref.dtype)


def _gmlp_sample(u, vg, za, g_v, w0, b0):
    m, w_a = u.shape
    full = pl.BlockSpec((m, w_a), lambda i: (0, 0))
    vec = pl.BlockSpec((1, w_a), lambda i: (0, 0))
    return pl.pallas_call(
        _gmlp_sample_body,
        out_shape=(jax.ShapeDtypeStruct((m, w_a), BF16), jax.ShapeDtypeStruct((m, w_a), F32)),
        grid=(1,),
        in_specs=[full, full, full, vec, vec, vec],
        out_specs=(full, full),
        compiler_params=_cparams("arbitrary"),
        name="gmlp_sample",
    )(u, vg, za, g_v, w0, b0)


def _log1p(y):
    w = 1.0 + y
    return jnp.where(w == 1.0, y, jnp.log(w) * y / (w - 1.0))


def _neg_expm1(x, u):
    safe = jnp.where((u == 1.0) | (u == 0.0), 0.5, u)
    return jnp.where(u == 1.0, -x, jnp.where(u == 0.0, 1.0, (1.0 - safe) * x / jnp.log(safe)))


def _log_sigmoid(x):
    return jnp.minimum(x, 0.0) - _log1p(jnp.exp(-jnp.abs(x)))


def _rg_gates(xc, wa_ref, ba_ref, wx_ref, bx_ref, lam_ref):
    xcb = xc.astype(BF16)
    bw = xc.shape[1] // N_B_BLOCKS
    ra, rx = [], []
    for n in range(N_B_BLOCKS):
        blk = xcb[:, n * bw:(n + 1) * bw]
        ra.append(jnp.dot(blk, wa_ref[n], preferred_element_type=F32))
        rx.append(jnp.dot(blk, wx_ref[n], preferred_element_type=F32))
    r = jax.nn.sigmoid(jnp.concatenate(ra, axis=1) + ba_ref[...])
    i = jax.nn.sigmoid(jnp.concatenate(rx, axis=1) + bx_ref[...])
    log_a = C_RG * r * _log_sigmoid(lam_ref[...])
    a = jnp.exp(log_a)
    b = jnp.sqrt(_neg_expm1(2.0 * log_a, a * a)) * i * xc
    return a, b


def _rglru_prompt_body(xb_ref, zb_ref, wc_ref, bc_ref, wa_ref, ba_ref, wx_ref, bx_ref, lam_ref,
                       o_ref, ht_ref, xp_scr, hs_scr, h_scr, *, tt):
    t = pl.program_id(1)

    @pl.when(t == 0)
    def _():
        xp_scr[0:SUBLANES, :] = jnp.zeros((SUBLANES, xp_scr.shape[1]), F32)
        h_scr[...] = jnp.zeros_like(h_scr)

    @pl.when(t > 0)
    def _():
        xp_scr[0:SUBLANES, :] = xp_scr[tt:tt + SUBLANES, :]

    xp_scr[SUBLANES:SUBLANES + tt, :] = xb_ref[...]
    xc = bc_ref[...]
    for j in range(CONV_W):
        off = SUBLANES - (CONV_W - 1) + j
        xc = xc + xp_scr[off:off + tt, :] * wc_ref[j:j + 1, :]
    a, b = _rg_gates(xc, wa_ref, ba_ref, wx_ref, bx_ref, lam_ref)
    rmod = lax.broadcasted_iota(jnp.int32, a.shape, 0) & (SUBLANES - 1)
    for s in (1, 2, 4):
        keep = rmod >= s
        a_prev = pltpu.roll(a, s, 0)
        b_prev = pltpu.roll(b, s, 0)
        b = jnp.where(keep, a * b_prev + b, b)
        a = jnp.where(keep, a * a_prev, a)
    h = h_scr[...]
    for g in range(tt // SUBLANES):
        rows = slice(g * SUBLANES, (g + 1) * SUBLANES)
        hg = a[rows] * h + b[rows]
        hs_scr[rows, :] = hg
        h = hg[SUBLANES - 1:SUBLANES]
    h_scr[...] = h
    o_ref[...] = (hs_scr[...] * zb_ref[...].astype(F32)).astype(o_ref.dtype)

    @pl.when(t == pl.num_programs(1) - 1)
    def _():
        ht_ref[...] = h


def _rglru_prompt(xb, zb, w_conv, b_conv, wa, ba, wx, bx, lam, *, n_seq, seq, tt=256):
    m, w_b = xb.shape
    tt = min(tt, seq)
    nt = seq // tt
    row_spec = pl.BlockSpec((tt, w_b), lambda b, t: (b * nt + t, 0))
    vec = pl.BlockSpec((1, w_b), lambda b, t: (0, 0))
    blk = pl.BlockSpec(wa.shape, lambda b, t: (0, 0, 0))
    return pl.pallas_call(
        functools.partial(_rglru_prompt_body, tt=tt),
        out_shape=(jax.ShapeDtypeStruct((m, w_b), BF16), jax.ShapeDtypeStruct((n_seq, 1, w_b), F32)),
        grid=(n_seq, nt),
        in_specs=[row_spec, row_spec,
                  pl.BlockSpec((CONV_W, w_b), lambda b, t: (0, 0)), vec,
                  blk, vec, blk, vec, vec],
        out_specs=(row_spec, pl.BlockSpec((None, 1, w_b), lambda b, t: (b, 0, 0))),
        scratch_shapes=[pltpu.VMEM((tt + SUBLANES, w_b), F32), pltpu.VMEM((tt, w_b), F32),
                        pltpu.VMEM((1, w_b), F32)],
        compiler_params=_cparams("parallel", "arbitrary"),
        name="rglru_prompt",
    )(xb, zb, w_conv, b_conv, wa, ba, wx, bx, lam)


def _rglru_sample_body(xb_ref, zb_ref, st_ref, h0_ref, wc_ref, bc_ref, wa_ref, ba_ref, wx_ref, bx_ref,
                       lam_ref, o_ref, h_ref):
    w_b = xb_ref.shape[1]
    xc = bc_ref[...]
    for j in range(CONV_W - 1):
        xc = xc + st_ref[:, j * w_b:(j + 1) * w_b] * wc_ref[j:j + 1, :]
    xc = xc + xb_ref[...] * wc_ref[CONV_W - 1:CONV_W, :]
    a, b = _rg_gates(xc, wa_ref, ba_ref, wx_ref, bx_ref, lam_ref)
    h = a * h0_ref[...] + b
    h_ref[...] = h
    o_ref[...] = (h * zb_ref[...].astype(F32)).astype(o_ref.dtype)


def _rglru_sample(xb, zb, st, h0, w_conv, b_conv, wa, ba, wx, bx, lam):
    m, w_b = xb.shape
    full = pl.BlockSpec((m, w_b), lambda i: (0, 0))
    vec = pl.BlockSpec((1, w_b), lambda i: (0, 0))
    blk = pl.BlockSpec(wa.shape, lambda i: (0, 0, 0))
    return pl.pallas_call(
        _rglru_sample_body,
        out_shape=(jax.ShapeDtypeStruct((m, w_b), BF16), jax.ShapeDtypeStruct((m, w_b), F32)),
        grid=(1,),
        in_specs=[full, full, pl.BlockSpec(st.shape, lambda i: (0, 0)), full,
                  pl.BlockSpec((CONV_W, w_b), lambda i: (0, 0)), vec, blk, vec, blk, vec, vec],
        out_specs=(full, full),
        compiler_params=_cparams("arbitrary"),
        name="rglru_sample",
    )(xb, zb, st, h0, w_conv, b_conv, wa, ba, wx, bx, lam)


def _order_key(score):
    score = jnp.where(score == 0.0, 0.0, score)
    bits = lax.bitcast_convert_type(score, jnp.int32)
    return jnp.where(bits < 0, bits ^ jnp.int32(0x7FFFFFFF), bits)


def _kth_largest(count_ge, k, shape):
    zero = jnp.zeros(shape, jnp.int32)
    base = jnp.where(count_ge(zero) >= k, zero, jnp.full(shape, INT_MIN, jnp.int32))

    def step(i, base):
        cand = base | jnp.left_shift(jnp.int32(1), 30 - i)
        return jnp.where(count_ge(cand) >= k, cand, base)

    return lax.fori_loop(0, 31, step, base)


def _dsa_prompt_block(span, j, top_k, q_ref, qi_ref, wt_ref, kidx_ref, kb_ref, vt_ref, zc_ref, o_ref, key_scr):
    kidx = kidx_ref[0:span, :]
    score = jnp.zeros((span, QBLOCK), F32)
    for hp in range(N_IDX_HEADS // 2):
        qi2 = qi_ref[2 * hp:2 * hp + 2].reshape(2 * QBLOCK, IDX_DIM)
        logit = lax.dot_general(kidx, qi2, (((1,), (1,)), ((), ())), preferred_element_type=F32)
        for u in range(2):
            h = 2 * hp + u
            score = score + jnp.maximum(logit[:, u * QBLOCK:(u + 1) * QBLOCK], 0.0) * wt_ref[h:h + 1, :]
    key_pos = lax.broadcasted_iota(jnp.int32, (span, QBLOCK), 0)
    q_pos = j * QBLOCK + lax.broadcasted_iota(jnp.int32, (span, QBLOCK), 1)
    key_scr[0:span, :] = jnp.where(key_pos <= q_pos, _order_key(score), INT_MIN)

    def count_ge(c):
        return jnp.sum(jnp.where(key_scr[0:span, :] >= c, 1.0, 0.0), axis=0, keepdims=True)

    thr = _kth_largest(count_ge, float(top_k), (1, QBLOCK))
    thr = jnp.maximum(thr, INT_MIN + 1)
    bias = jnp.where(key_scr[0:span, :] >= thr, 0.0, -jnp.inf)
    group = N_HEADS // N_KV
    bias_g = jnp.concatenate([bias] * group, axis=1)
    for n in range(N_KV):
        heads = range(n * group, (n + 1) * group)
        qs = jnp.concatenate([q_ref[:, h * HEAD_DIM:(h + 1) * HEAD_DIM] for h in heads], axis=0)
        kn = kb_ref[0:span, n * HEAD_DIM:(n + 1) * HEAD_DIM]
        st = lax.dot_general(kn, qs, (((1,), (1,)), ((), ())), preferred_element_type=F32) + bias_g
        mx = jnp.max(st, axis=0, keepdims=True)
        p = jnp.exp(st - mx)
        den = jnp.sum(p, axis=0, keepdims=True)
        vnt = vt_ref[n * HEAD_DIM:(n + 1) * HEAD_DIM, 0:span]
        ot = jnp.dot(vnt, p.astype(BF16), preferred_element_type=F32) / den
        for g, h in enumerate(heads):
            cols = slice(h * HEAD_DIM, (h + 1) * HEAD_DIM)
            o = ot[:, g * QBLOCK:(g + 1) * QBLOCK].T
            o_ref[:, cols] = (o * zc_ref[:, cols].astype(F32)).astype(o_ref.dtype)


def _dsa_prompt_body(q_ref, qi_ref, wt_ref, kidx_ref, kb_ref, vt_ref, zc_ref, o_ref, key_scr, *, seq, top_k,
                     n_span):
    j = pl.program_id(1)
    step = seq // n_span
    bucket = (j * QBLOCK) // step
    for s in range(n_span):
        @pl.when(bucket == s)
        def _(s=s):
            _dsa_prompt_block((s + 1) * step, j, top_k, q_ref, qi_ref, wt_ref, kidx_ref, kb_ref, vt_ref,
                              zc_ref, o_ref, key_scr)


def _dsa_prompt(q, qi, wt, kidx, kb, vt, zc, *, n_seq, seq):
    m, hd = q.shape
    nq = seq // QBLOCK
    top_k = min(TOPK_MAX, seq // 4)
    n_span = min(4, nq)
    kvw = N_KV * HEAD_DIM
    row_spec = pl.BlockSpec((QBLOCK, hd), lambda b, j: (b * nq + j, 0))
    return pl.pallas_call(
        functools.partial(_dsa_prompt_body, seq=seq, top_k=top_k, n_span=n_span),
        out_shape=jax.ShapeDtypeStruct((m, hd), BF16),
        grid=(n_seq, nq),
        in_specs=[row_spec,
                  pl.BlockSpec((None, N_IDX_HEADS, QBLOCK, IDX_DIM), lambda b, j: (b, 0, j, 0)),
                  pl.BlockSpec((None, N_IDX_HEADS, QBLOCK), lambda b, j: (b, 0, j)),
                  pl.BlockSpec((None, seq, IDX_DIM), lambda b, j: (b, 0, 0)),
                  pl.BlockSpec((None, seq, kvw), lambda b, j: (b, 0, 0)),
                  pl.BlockSpec((None, kvw, seq), lambda b, j: (b, 0, 0)),
                  row_spec],
        out_specs=row_spec,
        scratch_shapes=[pltpu.VMEM((seq, QBLOCK), jnp.int32)],
        compiler_params=_cparams("parallel", "arbitrary"),
        name="dsa_prompt",
    )(q, qi, wt, kidx, kb, vt, zc)


def _page_copy(cache_hbm, layer, page, dst, sem):
    return pltpu.make_async_copy(cache_hbm.at[layer, page], dst, sem)


def _dsa_sample_select_body(pt_ref, qi_ref, wcol_ref, knew_ref, ckidx_hbm, bias_ref, buf, sem, sc_scr, scn_scr,
                            *, layer, group, n_pages, top_k):
    step = pl.program_id(0)
    n_steps = pl.num_programs(0)
    past = n_pages * PAGE_SIZE

    def copies(step_idx, slot):
        out = []
        for i in range(group):
            for p in range(n_pages):
                page = pt_ref[step_idx * group + i, p]
                dst = buf.at[slot, i, pl.ds(p * PAGE_SIZE, PAGE_SIZE), :]
                out.append(_page_copy(ckidx_hbm, layer, page, dst, sem.at[slot]))
        return out

    @pl.when(step == 0)
    def _():
        for c in copies(0, 0):
            c.start()

    @pl.when(step + 1 < n_steps)
    def _():
        for c in copies(step + 1, (step + 1) % 2):
            c.start()

    slot = step % 2
    for c in copies(step, slot):
        c.wait()

    rows, news = [], []
    for i in range(group):
        qi = qi_ref[i]
        wcol = wcol_ref[i]
        keys = buf[slot, i].astype(BF16)
        logit = lax.dot_general(qi, keys, (((1,), (1,)), ((), ())), preferred_element_type=F32)
        rows.append(jnp.sum(jnp.maximum(logit, 0.0) * wcol, axis=0, keepdims=True))
        knew = knew_ref[i].astype(BF16).astype(F32)
        lnew = jnp.sum(qi.astype(F32) * knew, axis=1, keepdims=True)
        news.append(jnp.sum(jnp.maximum(lnew, 0.0) * wcol, axis=0, keepdims=True))
    r0 = pl.multiple_of(step * group, group)
    sc_scr[pl.ds(r0, group), :] = jnp.concatenate(rows, axis=0)
    scn_scr[pl.ds(r0, group), :] = jnp.broadcast_to(jnp.concatenate(news, axis=0), (group, LANES))

    @pl.when(step == n_steps - 1)
    def _():
        key = _order_key(sc_scr[...])
        key_new = _order_key(scn_scr[...])
        first = lax.broadcasted_iota(jnp.int32, key_new.shape, 1) == 0

        def count_ge(c):
            cnt = jnp.sum(jnp.where(key >= c, 1.0, 0.0), axis=1, keepdims=True)
            return cnt + jnp.sum(jnp.where(first & (key_new >= c), 1.0, 0.0), axis=1, keepdims=True)

        thr = _kth_largest(count_ge, float(top_k), (key.shape[0], 1))
        bias_ref[:, 0:past] = jnp.where(key >= thr, 0.0, -jnp.inf)
        bias_ref[:, past:past + LANES] = jnp.where(first & (key_new >= thr), 0.0, -jnp.inf)


def _dsa_sample_select(page_table, qi, wcol, knew, cache_kidx, *, layer, group=8):
    n_seq, n_pages = page_table.shape
    past = n_pages * PAGE_SIZE
    top_k = min(TOPK_MAX, (past + 1) // 4)
    assert n_seq % group == 0
    grid_spec = pltpu.PrefetchScalarGridSpec(
        num_scalar_prefetch=1,
        grid=(n_seq // group,),
        in_specs=[pl.BlockSpec((group, N_IDX_HEADS, IDX_DIM), lambda s, pt: (s, 0, 0)),
                  pl.BlockSpec((group, N_IDX_HEADS, 1), lambda s, pt: (s, 0, 0)),
                  pl.BlockSpec((group, 1, IDX_DIM), lambda s, pt: (s, 0, 0)),
                  pl.BlockSpec(memory_space=pl.ANY)],
        out_specs=pl.BlockSpec((n_seq, past + LANES), lambda s, pt: (0, 0)),
        scratch_shapes=[pltpu.VMEM((2, group, past, IDX_DIM), F32),
                        pltpu.SemaphoreType.DMA((2,)),
                        pltpu.VMEM((n_seq, past), F32),
                        pltpu.VMEM((n_seq, LANES), F32)],
    )
    return pl.pallas_call(
        functools.partial(_dsa_sample_select_body, layer=layer, group=group, n_pages=n_pages, top_k=top_k),
        out_shape=jax.ShapeDtypeStruct((n_seq, past + LANES), F32),
        grid_spec=grid_spec,
        compiler_params=_cparams("arbitrary"),
        name="dsa_sample_select",
    )(page_table, qi, wcol, knew, cache_kidx)


def _dsa_sample_attend_body(pt_ref, qblk_ref, bias_ref, knew_ref, vnew_ref, zc_ref, ck_hbm, cv_hbm, o_ref,
                            kbuf, vbuf, sem, *, layer, n_pages):
    b = pl.program_id(0)
    n_seq = pl.num_programs(0)
    past = n_pages * PAGE_SIZE

    def copies(seq_idx, slot):
        out = []
        for p in range(n_pages):
            page = pt_ref[seq_idx, p]
            rows = pl.ds(p * PAGE_SIZE, PAGE_SIZE)
            out.append(_page_copy(ck_hbm, layer, page, kbuf.at[slot, rows, :], sem.at[0, slot]))
            out.append(_page_copy(cv_hbm, layer, page, vbuf.at[slot, rows, :], sem.at[1, slot]))
        return out

    @pl.when(b == 0)
    def _():
        for c in copies(0, 0):
            c.start()

    @pl.when(b + 1 < n_seq)
    def _():
        for c in copies(b + 1, (b + 1) % 2):
            c.start()

    slot = b % 2
    for c in copies(b, slot):
        c.wait()

    qblk = qblk_ref[...]
    qf = qblk.astype(F32)
    kb = kbuf[slot].astype(BF16)
    s = lax.dot_general(qblk, kb, (((1,), (1,)), ((), ())), preferred_element_type=F32) + bias_ref[:, 0:past]
    knew = knew_ref[...].astype(BF16).astype(F32)
    s_new = jnp.sum(qf * knew, axis=1, keepdims=True) + bias_ref[:, past:past + 1]
    mx = jnp.maximum(jnp.max(s, axis=1, keepdims=True), s_new)
    p = jnp.exp(s - mx)
    p_new = jnp.exp(s_new - mx)
    den = jnp.sum(p, axis=1, keepdims=True) + p_new
    vb = vbuf[slot].astype(BF16)
    vnew = vnew_ref[...].astype(BF16).astype(F32)
    o = jnp.dot(p.astype(BF16), vb, preferred_element_type=F32) + p_new.astype(BF16).astype(F32) * vnew
    o = o / den
    group = N_HEADS // N_KV
    own = [o[n * group:(n + 1) * group, n * HEAD_DIM:(n + 1) * HEAD_DIM] for n in range(N_KV)]
    o_ref[...] = (jnp.concatenate(own, axis=0) * zc_ref[...].astype(F32)).astype(o_ref.dtype)


def _dsa_sample_attend(page_table, qblk, bias, knew, vnew, zc, cache_k, cache_v, *, layer):
    n_seq, n_pages = page_table.shape
    past = n_pages * PAGE_SIZE
    kvw = N_KV * HEAD_DIM
    grid_spec = pltpu.PrefetchScalarGridSpec(
        num_scalar_prefetch=1,
        grid=(n_seq,),
        in_specs=[pl.BlockSpec((None, N_HEADS, kvw), lambda b, pt: (b, 0, 0)),
                  pl.BlockSpec((None, 1, past + LANES), lambda b, pt: (b, 0, 0)),
                  pl.BlockSpec((None, 1, kvw), lambda b, pt: (b, 0, 0)),
                  pl.BlockSpec((None, 1, kvw), lambda b, pt: (b, 0, 0)),
                  pl.BlockSpec((None, N_HEADS, HEAD_DIM), lambda b, pt: (b, 0, 0)),
                  pl.BlockSpec(memory_space=pl.ANY),
                  pl.BlockSpec(memory_space=pl.ANY)],
        out_specs=pl.BlockSpec((None, N_HEADS, HEAD_DIM), lambda b, pt: (b, 0, 0)),
        scratch_shapes=[pltpu.VMEM((2, past, kvw), F32), pltpu.VMEM((2, past, kvw), F32),
                        pltpu.SemaphoreType.DMA((2, 2))],
    )
    return pl.pallas_call(
        functools.partial(_dsa_sample_attend_body, layer=layer, n_pages=n_pages),
        out_shape=jax.ShapeDtypeStruct((n_seq, N_HEADS, HEAD_DIM), BF16),
        grid_spec=grid_spec,
        compiler_params=_cparams("arbitrary"),
        name="dsa_sample_attend",
    )(page_table, qblk, bias, knew, vnew, zc, cache_k, cache_v)


def _merge_body(a_ref, b_ref, c_ref, wa_ref, wb_ref, wc_ref, ga_ref, gb_ref, gc_ref, o_ref):
    acc = ga_ref[...].astype(F32) * jnp.dot(a_ref[...], wa_ref[...], preferred_element_type=F32)
    acc = acc + gb_ref[...].astype(F32) * jnp.dot(b_ref[...], wb_ref[...], preferred_element_type=F32)
    acc = acc + gc_ref[...].astype(F32) * jnp.dot(c_ref[...], wc_ref[...], preferred_element_type=F32)
    o_ref[...] = acc.astype(o_ref.dtype)


def _merge(a_in, b_in, c_in, w_pa, w_pb, w_pc, gm, *, tm, tn=512):
    m, k = a_in.shape
    d = w_pa.shape[1]
    tm, tn = min(tm, m), min(tn, d)
    nj = d // tn
    x_spec = pl.BlockSpec((tm, k), lambda i, j: (i, 0))
    w_spec = pl.BlockSpec((k, tn), lambda i, j: (0, j))
    g_specs = [pl.BlockSpec((tm, tn), lambda i, j, br=br: (i, br * nj + j)) for br in range(N_BRANCH)]
    return pl.pallas_call(
        _merge_body,
        out_shape=jax.ShapeDtypeStruct((m, d), BF16),
        grid=(m // tm, nj),
        in_specs=[x_spec, x_spec, x_spec, w_spec, w_spec, w_spec] + g_specs,
        out_specs=pl.BlockSpec((tm, tn), lambda i, j: (i, j)),
        compiler_params=_cparams("parallel", "arbitrary"),
        name="merge",
    )(a_in, b_in, c_in, w_pa, w_pb, w_pc, gm, gm, gm)


def _out_body(mg_ref, w_ref, x_ref, g_ref, gate_ref, o_ref):
    o = jnp.dot(mg_ref[...], w_ref[...], preferred_element_type=F32)
    y = o * lax.rsqrt(jnp.mean(o * o, axis=-1, keepdims=True) + EPS) * g_ref[...]
    o_ref[...] = x_ref[...] + gate_ref[...] * y


def _out_proj(merged, w_out, x, g_post, mod, *, per_row, rows_per_seq, tm):
    m, d = x.shape
    tm = min(tm, m)
    row_spec = pl.BlockSpec((tm, d), lambda i: (i, 0))
    return pl.pallas_call(
        _out_body,
        out_shape=jax.ShapeDtypeStruct((m, d), F32),
        grid=(m // tm,),
        in_specs=[row_spec, pl.BlockSpec((d, d), lambda i: (0, 0)), row_spec,
                  pl.BlockSpec((1, d), lambda i: (0, 0)),
                  _mod_spec(per_row, 2, tm, d, rows_per_seq)],
        out_specs=row_spec,
        compiler_params=_cparams("parallel"),
        name="out_proj",
    )(merged, w_out, x, g_post, mod)


def _split_points(d, w_a, w_b):
    sizes = (w_a, w_a, w_a, w_b, w_b, N_HEADS * HEAD_DIM, N_KV * HEAD_DIM, N_KV * HEAD_DIM,
             N_IDX_HEADS * IDX_DIM, IDX_DIM, N_IDX_HEADS, N_HEADS * HEAD_DIM, N_BRANCH * d)
    offs = [0]
    for s in sizes:
        offs.append(offs[-1] + s)
    return offs


def _project_all(h, w_in, offs, tm):
    def seg(i, **kw):
        return _matmul(h, w_in[:, offs[i]:offs[i + 1]].astype(BF16), tm=tm, tn=512, **kw)

    out = {
        "u": seg(0, out_dtype=BF16, act="gelu"),
        "vg": seg(1, out_dtype=F32, act="gelu"),
        "za": seg(2, out_dtype=BF16, act="silu"),
        "xb": seg(3, out_dtype=F32),
        "zb": seg(4, out_dtype=BF16, act="silu"),
        "q": seg(5, out_dtype=BF16, scale=HEAD_DIM ** -0.5),
        "k": seg(6, out_dtype=F32),
        "v": seg(7, out_dtype=F32),
        "qi": seg(8, out_dtype=BF16, scale=IDX_DIM ** -0.5),
        "zc": seg(11, out_dtype=BF16, act="silu"),
        "gm": seg(12, out_dtype=BF16, act="sigmoid"),
    }
    w_small = w_in[:, offs[9]:offs[11]].astype(BF16)
    w_small = jnp.pad(w_small, ((0, 0), (0, LANES - w_small.shape[1])))
    small = _matmul(h, w_small, out_dtype=F32, tm=tm, tn=LANES)
    out["kidx"] = small[:, :IDX_DIM]
    out["widx"] = small[:, IDX_DIM:IDX_DIM + N_IDX_HEADS] * (N_IDX_HEADS ** -0.5)
    return out


def _layer_weights(l, w):
    d = w["w_in"].shape[1]
    w_a = w["g_v"].shape[1]
    return {
        "g_pre": w["g_pre"][l][None], "g_v": w["g_v"][l][None], "g_post": w["g_post"][l][None],
        "w_in": w["w_in"][l], "w_s": w["w_s"][l], "b_s_t": w["b_s"][l].T,
        "ws0": jnp.repeat(w["w_s"][l][:, 0, 0], w_a // N_A_GROUPS)[None],
        "bs0": jnp.repeat(w["b_s"][l][:, 0], w_a // N_A_GROUPS)[None],
        "w_conv": w["w_conv"][l], "b_conv": w["b_conv"][l][None],
        "w_rg_a": w["w_rg_a"][l].astype(BF16), "b_rg_a": w["b_rg_a"][l][None],
        "w_rg_x": w["w_rg_x"][l].astype(BF16), "b_rg_x": w["b_rg_x"][l][None],
        "lam": w["lam"][l][None],
        "w_pa": w["w_pa"][l].astype(BF16), "w_pb": w["w_pb"][l].astype(BF16),
        "w_pc": w["w_pc"][l].astype(BF16), "w_out": w["w_out"][l].astype(BF16),
        "d": d,
    }


def _prompt_layer(x, mod, lw, offs, *, n_seq, seq):
    tm = min(1024, seq)
    common = dict(per_row=False, rows_per_seq=seq)
    h = _prenorm(x, lw["g_pre"], mod, tm=min(512, seq), **common)
    pr = _project_all(h, lw["w_in"], offs, tm)
    a_in = _gmlp_prompt(pr["u"], pr["vg"], pr["za"], lw["g_v"], lw["w_s"], lw["b_s_t"])
    b_in, h_t = _rglru_prompt(pr["xb"], pr["zb"], lw["w_conv"], lw["b_conv"], lw["w_rg_a"], lw["b_rg_a"],
                              lw["w_rg_x"], lw["b_rg_x"], lw["lam"], n_seq=n_seq, seq=seq)
    kvw = N_KV * HEAD_DIM
    qi = pr["qi"].reshape(n_seq, seq, N_IDX_HEADS, IDX_DIM).transpose(0, 2, 1, 3)
    wt = pr["widx"].reshape(n_seq, seq, N_IDX_HEADS).transpose(0, 2, 1)
    kidx_b = pr["kidx"].astype(BF16).reshape(n_seq, seq, IDX_DIM)
    kb = pr["k"].astype(BF16).reshape(n_seq, seq, kvw)
    vt = pr["v"].astype(BF16).reshape(n_seq, seq, kvw).transpose(0, 2, 1)
    c_in = _dsa_prompt(pr["q"], qi, wt, kidx_b, kb, vt, pr["zc"], n_seq=n_seq, seq=seq)
    merged = _merge(a_in, b_in, c_in, lw["w_pa"], lw["w_pb"], lw["w_pc"], pr["gm"], tm=min(512, seq))
    y = _out_proj(merged, lw["w_out"], x, lw["g_post"], mod, tm=min(512, seq), **common)
    w_b = pr["xb"].shape[1]
    state = (pr["k"].reshape(n_seq, seq, N_KV, HEAD_DIM), pr["v"].reshape(n_seq, seq, N_KV, HEAD_DIM),
             pr["kidx"].reshape(n_seq, seq, IDX_DIM),
             pr["xb"].reshape(n_seq, seq, w_b)[:, seq - (CONV_W - 1):],
             h_t.reshape(n_seq, w_b))
    return y, state


def _sample_layer(x, mod, lw, offs, layer, conv_state, h0, page_table, cache_k, cache_v, cache_kidx):
    n_seq = x.shape[0]
    common = dict(per_row=True, rows_per_seq=1)
    h = _prenorm(x, lw["g_pre"], mod, tm=n_seq, **common)
    pr = _project_all(h, lw["w_in"], offs, n_seq)
    a_in, v_a = _gmlp_sample(pr["u"], pr["vg"], pr["za"], lw["g_v"], lw["ws0"], lw["bs0"])
    w_b = pr["xb"].shape[1]
    b_in, h_new = _rglru_sample(pr["xb"], pr["zb"], conv_state.reshape(n_seq, (CONV_W - 1) * w_b), h0,
                                lw["w_conv"], lw["b_conv"], lw["w_rg_a"], lw["b_rg_a"], lw["w_rg_x"],
                                lw["b_rg_x"], lw["lam"])
    kvw = N_KV * HEAD_DIM
    qi = pr["qi"].reshape(n_seq, N_IDX_HEADS, IDX_DIM)
    wcol = pr["widx"].reshape(n_seq, N_IDX_HEADS, 1)
    bias = _dsa_sample_select(page_table, qi, wcol, pr["kidx"].reshape(n_seq, 1, IDX_DIM), cache_kidx,
                              layer=layer)
    head_kv = jnp.arange(N_HEADS)[:, None] // (N_HEADS // N_KV) == jnp.arange(N_KV)[None, :]
    q4 = pr["q"].reshape(n_seq, N_HEADS, 1, HEAD_DIM)
    qblk = jnp.where(head_kv[None, :, :, None], q4, jnp.zeros_like(q4)).reshape(n_seq, N_HEADS, kvw)
    c_in = _dsa_sample_attend(page_table, qblk, bias.reshape(n_seq, 1, -1), pr["k"].reshape(n_seq, 1, kvw),
                              pr["v"].reshape(n_seq, 1, kvw), pr["zc"].reshape(n_seq, N_HEADS, HEAD_DIM),
                              cache_k, cache_v, layer=layer)
    merged = _merge(a_in, b_in, c_in.reshape(n_seq, N_HEADS * HEAD_DIM), lw["w_pa"], lw["w_pb"], lw["w_pc"],
                    pr["gm"], tm=n_seq)
    y = _out_proj(merged, lw["w_out"], x, lw["g_post"], mod, tm=n_seq, **common)
    new_conv = jnp.concatenate([conv_state[:, 1:], pr["xb"][:, None, :]], axis=1)
    state = (pr["k"].reshape(n_seq, 1, N_KV, HEAD_DIM), pr["v"].reshape(n_seq, 1, N_KV, HEAD_DIM),
             pr["kidx"].reshape(n_seq, 1, IDX_DIM), new_conv, h_new, v_a.reshape(n_seq, 1, -1))
    return y, state


def kernel(x_prompt, x_sample, c_prompt, c_sample, cache_k, cache_v, cache_kidx, state_conv, state_h, page_table, w_mod, b_mod, g_pre, w_in, g_v, w_s, b_s, w_conv, b_conv, w_rg_a, b_rg_a, w_rg_x, b_rg_x, lam, w_pa, w_pb, w_pc, w_out, g_post):
    n_seq, seq, d = x_prompt.shape
    n_dec, dec_seq, _ = x_sample.shape
    assert dec_seq == 1 and seq % QBLOCK == 0
    depth = w_in.shape[0]
    w_a, w_b = g_v.shape[1], lam.shape[1]
    offs = _split_points(d, w_a, w_b)
    weights = dict(w_mod=w_mod, b_mod=b_mod, g_pre=g_pre, w_in=w_in, g_v=g_v, w_s=w_s, b_s=b_s, w_conv=w_conv,
                   b_conv=b_conv, w_rg_a=w_rg_a, b_rg_a=b_rg_a, w_rg_x=w_rg_x, b_rg_x=b_rg_x, lam=lam,
                   w_pa=w_pa, w_pb=w_pb, w_pc=w_pc, w_out=w_out, g_post=g_post)
    n_phys = cache_k.shape[1]
    kvw = N_KV * HEAD_DIM
    ck = cache_k.reshape(depth, n_phys, PAGE_SIZE, kvw)
    cv = cache_v.reshape(depth, n_phys, PAGE_SIZE, kvw)

    n_c = n_seq + n_dec
    pad = -n_c % SUBLANES
    c_all = jnp.concatenate([c_prompt, c_sample, jnp.zeros((pad, d), F32)], axis=0)

    y_p = x_prompt.reshape(n_seq * seq, d)
    y_s = x_sample.reshape(n_dec, d)
    st_p, st_s = [], []
    for l in range(depth):
        lw = _layer_weights(l, weights)
        mod = _modulation(c_all, w_mod[l].astype(BF16), b_mod[l][None])
        mod_p = mod[:n_seq].reshape(n_seq, 1, 3 * d)
        mod_s = mod[n_seq:n_c]
        y_p, sp = _prompt_layer(y_p, mod_p, lw, offs, n_seq=n_seq, seq=seq)
        y_s, ss = _sample_layer(y_s, mod_s, lw, offs, l, state_conv[l], state_h[l], page_table, ck, cv,
                                cache_kidx)
        st_p.append(sp)
        st_s.append(ss)
    outs_p = [jnp.stack([s[i] for s in st_p]) for i in range(5)]
    outs_s = [jnp.stack([s[i] for s in st_s]) for i in range(6)]
    return (y_p.reshape(n_seq, seq, d), y_s.reshape(n_dec, 1, d), *outs_p, *outs_s)
```

```python
import functools
import math

import jax
import jax.numpy as jnp
from jax import lax
from jax.experimental import pallas as pl
from jax.experimental.pallas import tpu as pltpu

F32 = jnp.float32
BF16 = jnp.bfloat16
INT_MIN = -(2 ** 31)
NEG_INF_PATTERN = -(2 ** 31) + 0x7FFFFF

N_A_GROUPS = 8
CHUNK = 128
N_B_BLOCKS = 16
CONV_W = 4
C_RG = 8.0
N_HEADS = 16
HEAD_DIM = 128
N_KV = 4
N_IDX_HEADS = 16
IDX_DIM = 64
TOPK_MAX = 256
QBLOCK = 128
PAGE_SIZE = 128
N_BRANCH = 3
EPS = 1e-6

V7X_VMEM_LIMIT_BYTES = 56 * 1024 * 1024
SUBLANES = 8
LANES = 128


def _cparams(*sem):
    return pltpu.CompilerParams(dimension_semantics=sem, vmem_limit_bytes=V7X_VMEM_LIMIT_BYTES)


def _silu(x):
    return x * jax.nn.sigmoid(x)


def _apply_act(act, x):
    if act == "gelu":
        return jax.nn.gelu(x)
    if act == "silu":
        return _silu(x)
    if act == "sigmoid":
        return jax.nn.sigmoid(x)
    assert act == "none"
    return x


def _mm_body(x_ref, w_ref, o_ref, *, act, scale):
    acc = jnp.dot(x_ref[...], w_ref[...], preferred_element_type=F32)
    if scale != 1.0:
        acc = acc * scale
    o_ref[...] = _apply_act(act, acc).astype(o_ref.dtype)


def _matmul(x, w, *, out_dtype, act="none", scale=1.0, tm, tn):
    m, k = x.shape
    n = w.shape[1]
    tm, tn = min(tm, m), min(tn, n)
    assert m % tm == 0 and n % tn == 0
    return pl.pallas_call(
        functools.partial(_mm_body, act=act, scale=scale),
        out_shape=jax.ShapeDtypeStruct((m, n), out_dtype),
        grid=(m // tm, n // tn),
        in_specs=[pl.BlockSpec((tm, k), lambda i, j: (i, 0)),
                  pl.BlockSpec((k, tn), lambda i, j: (0, j))],
        out_specs=pl.BlockSpec((tm, tn), lambda i, j: (i, j)),
        compiler_params=_cparams("parallel", "arbitrary"),
        name="proj",
    )(x, w)


def _mod_body(c_ref, w_ref, b_ref, o_ref):
    s = _silu(c_ref[...]).astype(BF16)
    o_ref[...] = jnp.dot(s, w_ref[...], preferred_element_type=F32) + b_ref[...]


def _modulation(c, w, b, *, tn=1024):
    m, k = c.shape
    n = w.shape[1]
    return pl.pallas_call(
        _mod_body,
        out_shape=jax.ShapeDtypeStruct((m, n), F32),
        grid=(n // tn,),
        in_specs=[pl.BlockSpec((m, k), lambda j: (0, 0)),
                  pl.BlockSpec((k, tn), lambda j: (0, j)),
                  pl.BlockSpec((1, tn), lambda j: (0, j))],
        out_specs=pl.BlockSpec((m, tn), lambda j: (0, j)),
        compiler_params=_cparams("arbitrary"),
        name="modulation",
    )(c, w, b)


def _mod_spec(per_row, which, tm, d, rows_per_seq):
    if per_row:
        return pl.BlockSpec((tm, d), lambda i, *_: (i, which))
    return pl.BlockSpec((None, 1, d), lambda i, *_: ((i * tm) // rows_per_seq, 0, which))


def _prenorm_body(x_ref, g_ref, sh_ref, sc_ref, o_ref):
    x = x_ref[...]
    y = x * lax.rsqrt(jnp.mean(x * x, axis=-1, keepdims=True) + EPS) * g_ref[...]
    o_ref[...] = (y * (1.0 + sc_ref[...]) + sh_ref[...]).astype(o_ref.dtype)


def _prenorm(x, g, mod, *, per_row, rows_per_seq, tm):
    m, d = x.shape
    tm = min(tm, m)
    return pl.pallas_call(
        _prenorm_body,
        out_shape=jax.ShapeDtypeStruct((m, d), BF16),
        grid=(m // tm,),
        in_specs=[pl.BlockSpec((tm, d), lambda i: (i, 0)),
                  pl.BlockSpec((1, d), lambda i: (0, 0)),
                  _mod_spec(per_row, 0, tm, d, rows_per_seq),
                  _mod_spec(per_row, 1, tm, d, rows_per_seq)],
        out_specs=pl.BlockSpec((tm, d), lambda i: (i, 0)),
        compiler_params=_cparams("parallel"),
        name="prenorm",
    )(x, g, mod, mod)


def _gmlp_prompt_body(u_ref, vg_ref, za_ref, gv_ref, ws_ref, bs_ref, o_ref):
    v = vg_ref[...]
    vn = v * lax.rsqrt(jnp.mean(v * v, axis=-1, keepdims=True) + EPS) * gv_ref[...]
    vnb = vn.astype(BF16)
    w_a = vn.shape[1]
    gw = w_a // N_A_GROUPS
    row = lax.broadcasted_iota(jnp.int32, (CHUNK, CHUNK), 0)
    col = lax.broadcasted_iota(jnp.int32, (CHUNK, CHUNK), 1)
    causal = col <= row
    for g in range(N_A_GROUPS):
        ws = jnp.where(causal, ws_ref[g], 0.0).astype(BF16)
        sl = slice(g * gw, (g + 1) * gw)
        s = jnp.dot(ws, vnb[:, sl], preferred_element_type=F32) + bs_ref[:, g:g + 1]
        o = u_ref[:, sl].astype(F32) * s * za_ref[:, sl].astype(F32)
        o_ref[:, sl] = o.astype(o_ref.dtype)


def _gmlp_prompt(u, vg, za, g_v, w_s, b_s_t):
    m, w_a = u.shape
    row_spec = pl.BlockSpec((CHUNK, w_a), lambda i: (i, 0))
    return pl.pallas_call(
        _gmlp_prompt_body,
        out_shape=jax.ShapeDtypeStruct((m, w_a), BF16),
        grid=(m // CHUNK,),
        in_specs=[row_spec, row_spec, row_spec,
                  pl.BlockSpec((1, w_a), lambda i: (0, 0)),
                  pl.BlockSpec(w_s.shape, lambda i: (0, 0, 0)),
                  pl.BlockSpec(b_s_t.shape, lambda i: (0, 0))],
        out_specs=row_spec,
        compiler_params=_cparams("parallel"),
        name="gmlp_prompt",
    )(u, vg, za, g_v, w_s, b_s_t)


def _gmlp_sample_body(u_ref, vg_ref, za_ref, gv_ref, w0_ref, b0_ref, o_ref, vn_ref):
    v = vg_ref[...]
    vn = v * lax.rsqrt(jnp.mean(v * v, axis=-1, keepdims=True) + EPS) * gv_ref[...]
    vn_ref[...] = vn
    s = vn * w0_ref[...] + b0_ref[...]
    o_ref[...] = (u_ref[...].astype(F32) * s * za_ref[...].astype(F32)).astype(o_ref.dtype)


def _gmlp_sample(u, vg, za, g_v, w0, b0):
    m, w_a = u.shape
    full = pl.BlockSpec((m, w_a), lambda i: (0, 0))
    vec = pl.BlockSpec((1, w_a), lambda i: (0, 0))
    return pl.pallas_call(
        _gmlp_sample_body,
        out_shape=(jax.ShapeDtypeStruct((m, w_a), BF16), jax.ShapeDtypeStruct((m, w_a), F32)),
        grid=(1,),
        in_specs=[full, full, full, vec, vec, vec],
        out_specs=(full, full),
        compiler_params=_cparams("arbitrary"),
        name="gmlp_sample",
    )(u, vg, za, g_v, w0, b0)


def _log1p(y):
    w = 1.0 + y
    return jnp.where(w == 1.0, y, jnp.log(w) * y / (w - 1.0))


def _neg_expm1(x, u):
    safe = jnp.where((u == 1.0) | (u == 0.0), 0.5, u)
    return jnp.where(u == 1.0, -x, jnp.where(u == 0.0, 1.0, (1.0 - safe) * x / jnp.log(safe)))


def _log_sigmoid(x):
    return jnp.minimum(x, 0.0) - _log1p(jnp.exp(-jnp.abs(x)))


def _rg_gates(xc, wa_ref, ba_ref, wx_ref, bx_ref, lam_ref):
    xcb = xc.astype(BF16)
    bw = xc.shape[1] // N_B_BLOCKS
    ra, rx = [], []
    for n in range(N_B_BLOCKS):
        blk = xcb[:, n * bw:(n + 1) * bw]
        ra.append(jnp.dot(blk, wa_ref[n], preferred_element_type=F32))
        rx.append(jnp.dot(blk, wx_ref[n], preferred_element_type=F32))
    r = jax.nn.sigmoid(jnp.concatenate(ra, axis=1) + ba_ref[...])
    i = jax.nn.sigmoid(jnp.concatenate(rx, axis=1) + bx_ref[...])
    log_a = C_RG * r * _log_sigmoid(lam_ref[...])
    a = jnp.exp(log_a)
    b = jnp.sqrt(_neg_expm1(2.0 * log_a, a * a)) * i * xc
    return a, b


def _rglru_prompt_body(xb_ref, zb_ref, wc_ref, bc_ref, wa_ref, ba_ref, wx_ref, bx_ref, lam_ref,
                       o_ref, ht_ref, xp_scr, hs_scr, h_scr, *, tt):
    t = pl.program_id(1)

    @pl.when(t == 0)
    def _():
        xp_scr[0:SUBLANES, :] = jnp.zeros((SUBLANES, xp_scr.shape[1]), F32)
        h_scr[...] = jnp.zeros_like(h_scr)

    @pl.when(t > 0)
    def _():
        xp_scr[0:SUBLANES, :] = xp_scr[tt:tt + SUBLANES, :]

    xp_scr[SUBLANES:SUBLANES + tt, :] = xb_ref[...]
    xc = bc_ref[...]
    for j in range(CONV_W):
        off = SUBLANES - (CONV_W - 1) + j
        xc = xc + xp_scr[off:off + tt, :] * wc_ref[j:j + 1, :]
    a, b = _rg_gates(xc, wa_ref, ba_ref, wx_ref, bx_ref, lam_ref)
    rmod = lax.broadcasted_iota(jnp.int32, a.shape, 0) & (SUBLANES - 1)
    for s in (1, 2, 4):
        keep = rmod >= s
        a_prev = pltpu.roll(a, s, 0)
        b_prev = pltpu.roll(b, s, 0)
        b = jnp.where(keep, a * b_prev + b, b)
        a = jnp.where(keep, a * a_prev, a)
    h = h_scr[...]
    for g in range(tt // SUBLANES):
        rows = slice(g * SUBLANES, (g + 1) * SUBLANES)
        hg = a[rows] * h + b[rows]
        hs_scr[rows, :] = hg
        h = hg[SUBLANES - 1:SUBLANES]
    h_scr[...] = h
    o_ref[...] = (hs_scr[...] * zb_ref[...].astype(F32)).astype(o_ref.dtype)

    @pl.when(t == pl.num_programs(1) - 1)
    def _():
        ht_ref[...] = h


def _rglru_prompt(xb, zb, w_conv, b_conv, wa, ba, wx, bx, lam, *, n_seq, seq, tt=256):
    m, w_b = xb.shape
    tt = min(tt, seq)
    nt = seq // tt
    row_spec = pl.BlockSpec((tt, w_b), lambda b, t: (b * nt + t, 0))
    vec = pl.BlockSpec((1, w_b), lambda b, t: (0, 0))
    blk = pl.BlockSpec(wa.shape, lambda b, t: (0, 0, 0))
    return pl.pallas_call(
        functools.partial(_rglru_prompt_body, tt=tt),
        out_shape=(jax.ShapeDtypeStruct((m, w_b), BF16), jax.ShapeDtypeStruct((n_seq, 1, w_b), F32)),
        grid=(n_seq, nt),
        in_specs=[row_spec, row_spec,
                  pl.BlockSpec((CONV_W, w_b), lambda b, t: (0, 0)), vec,
                  blk, vec, blk, vec, vec],
        out_specs=(row_spec, pl.BlockSpec((None, 1, w_b), lambda b, t: (b, 0, 0))),
        scratch_shapes=[pltpu.VMEM((tt + SUBLANES, w_b), F32), pltpu.VMEM((tt, w_b), F32),
                        pltpu.VMEM((1, w_b), F32)],
        compiler_params=_cparams("parallel", "arbitrary"),
        name="rglru_prompt",
    )(xb, zb, w_conv, b_conv, wa, ba, wx, bx, lam)


def _rglru_sample_body(xb_ref, zb_ref, st_ref, h0_ref, wc_ref, bc_ref, wa_ref, ba_ref, wx_ref, bx_ref,
                       lam_ref, o_ref, h_ref):
    xc = bc_ref[...]
    for j in range(CONV_W - 1):
        xc = xc + st_ref[j] * wc_ref[j:j + 1, :]
    xc = xc + xb_ref[...] * wc_ref[CONV_W - 1:CONV_W, :]
    a, b = _rg_gates(xc, wa_ref, ba_ref, wx_ref, bx_ref, lam_ref)
    h = a * h0_ref[...] + b
    h_ref[...] = h
    o_ref[...] = (h * zb_ref[...].astype(F32)).astype(o_ref.dtype)


def _rglru_sample(xb, zb, st, h0, w_conv, b_conv, wa, ba, wx, bx, lam):
    m, w_b = xb.shape
    full = pl.BlockSpec((m, w_b), lambda i: (0, 0))
    vec = pl.BlockSpec((1, w_b), lambda i: (0, 0))
    blk = pl.BlockSpec(wa.shape, lambda i: (0, 0, 0))
    return pl.pallas_call(
        _rglru_sample_body,
        out_shape=(jax.ShapeDtypeStruct((m, w_b), BF16), jax.ShapeDtypeStruct((m, w_b), F32)),
        grid=(1,),
        in_specs=[full, full, pl.BlockSpec(st.shape, lambda i: (0, 0, 0)), full,
                  pl.BlockSpec((CONV_W, w_b), lambda i: (0, 0)), vec, blk, vec, blk, vec, vec],
        out_specs=(full, full),
        compiler_params=_cparams("arbitrary"),
        name="rglru_sample",
    )(xb, zb, st, h0, w_conv, b_conv, wa, ba, wx, bx, lam)


def _pattern_to_float(pattern):
    pattern = jnp.maximum(pattern, jnp.int32(NEG_INF_PATTERN))
    bits = jnp.where(pattern < 0, pattern ^ jnp.int32(0x7FFFFFFF), pattern)
    return lax.bitcast_convert_type(bits, F32)


def _kth_largest(count_ge, k, shape):
    zero = jnp.zeros(shape, jnp.int32)
    base = jnp.where(count_ge(_pattern_to_float(zero)) >= k, zero, jnp.full(shape, INT_MIN, jnp.int32))

    def step(i, base):
        cand = base | jnp.left_shift(jnp.int32(1), 30 - i)
        return jnp.where(count_ge(_pattern_to_float(cand)) >= k, cand, base)

    return _pattern_to_float(lax.fori_loop(0, 31, step, base))


def _any_true(cond):
    return jnp.max(jnp.where(cond, 1.0, 0.0)) > 0.5


def _dsa_prompt_block(span, j, top_k, q_ref, qi_ref, wt_ref, kidx_ref, kb_ref, vt_ref, zc_ref, o_ref, sc_scr,
                      bias_scr):
    kidx = kidx_ref[0:span, :]
    score = jnp.zeros((span, QBLOCK), F32)
    for hp in range(N_IDX_HEADS // 2):
        qi2 = qi_ref[2 * hp:2 * hp + 2].reshape(2 * QBLOCK, IDX_DIM)
        logit = lax.dot_general(kidx, qi2, (((1,), (1,)), ((), ())), preferred_element_type=F32)
        for u in range(2):
            h = 2 * hp + u
            score = score + jnp.maximum(logit[:, u * QBLOCK:(u + 1) * QBLOCK], 0.0) * wt_ref[h:h + 1, :]
    key_pos = lax.broadcasted_iota(jnp.int32, (span, QBLOCK), 0)
    q_pos = j * QBLOCK + lax.broadcasted_iota(jnp.int32, (span, QBLOCK), 1)
    sc_scr[0:span, :] = jnp.where(key_pos <= q_pos, score, -jnp.inf)
    k = float(top_k)

    def count(pred):
        return jnp.sum(jnp.where(pred, 1.0, 0.0), axis=0, keepdims=True)

    thr = _kth_largest(lambda c: count(sc_scr[0:span, :] >= c), k, (1, QBLOCK))
    sc = sc_scr[0:span, :]
    causal = key_pos <= q_pos
    sel = jnp.where(causal, jnp.where(sc >= thr, 1.0, 0.0), 0.0)
    bias_scr[0:span, :] = jnp.where(sel > 0.5, 0.0, -jnp.inf)

    @pl.when(_any_true(jnp.sum(sel, axis=0, keepdims=True) > k))
    def _():
        sc = sc_scr[0:span, :]
        above = jnp.where(causal, jnp.where(sc > thr, 1.0, 0.0), 0.0)
        tied = jnp.where(causal, jnp.where(sc == thr, 1.0, 0.0), 0.0)
        need = k - jnp.sum(above, axis=0, keepdims=True)
        tied_b = tied.astype(BF16)
        rows = math.gcd(span, 2 * LANES)
        for r0 in range(0, span, rows):
            r = r0 + lax.broadcasted_iota(jnp.int32, (rows, span), 0)
            c = lax.broadcasted_iota(jnp.int32, (rows, span), 1)
            earlier = jnp.where(c < r, 1.0, 0.0).astype(BF16)
            rank = jnp.dot(earlier, tied_b, preferred_element_type=F32)
            keep = above[r0:r0 + rows] + tied[r0:r0 + rows] * jnp.where(rank < need, 1.0, 0.0)
            bias_scr[r0:r0 + rows, :] = jnp.where(keep > 0.5, 0.0, -jnp.inf)

    bias = bias_scr[0:span, :]
    group = N_HEADS // N_KV
    bias_g = jnp.concatenate([bias] * group, axis=1)
    for n in range(N_KV):
        heads = range(n * group, (n + 1) * group)
        qs = jnp.concatenate([q_ref[:, h * HEAD_DIM:(h + 1) * HEAD_DIM] for h in heads], axis=0)
        kn = kb_ref[0:span, n * HEAD_DIM:(n + 1) * HEAD_DIM]
        st = lax.dot_general(kn, qs, (((1,), (1,)), ((), ())), preferred_element_type=F32) + bias_g
        mx = jnp.max(st, axis=0, keepdims=True)
        p = jnp.exp(st - mx)
        den = jnp.sum(p, axis=0, keepdims=True)
        vnt = vt_ref[n * HEAD_DIM:(n + 1) * HEAD_DIM, 0:span]
        ot = jnp.dot(vnt, p.astype(BF16), preferred_element_type=F32) / den
        for g, h in enumerate(heads):
            cols = slice(h * HEAD_DIM, (h + 1) * HEAD_DIM)
            o = ot[:, g * QBLOCK:(g + 1) * QBLOCK].T
            o_ref[:, cols] = (o * zc_ref[:, cols].astype(F32)).astype(o_ref.dtype)


def _dsa_prompt_body(q_ref, qi_ref, wt_ref, kidx_ref, kb_ref, vt_ref, zc_ref, o_ref, sc_scr, bias_scr, *, seq,
                     top_k, n_span):
    j = pl.program_id(1)
    step = seq // n_span
    bucket = (j * QBLOCK) // step
    for s in range(n_span):
        @pl.when(bucket == s)
        def _(s=s):
            _dsa_prompt_block((s + 1) * step, j, top_k, q_ref, qi_ref, wt_ref, kidx_ref, kb_ref, vt_ref,
                              zc_ref, o_ref, sc_scr, bias_scr)


def _dsa_prompt(q, qi, wt, kidx, kb, vt, zc, *, n_seq, seq):
    m, hd = q.shape
    nq = seq // QBLOCK
    top_k = min(TOPK_MAX, seq // 4)
    n_span = min(8, nq)
    kvw = N_KV * HEAD_DIM
    row_spec = pl.BlockSpec((QBLOCK, hd), lambda b, j: (b * nq + j, 0))
    return pl.pallas_call(
        functools.partial(_dsa_prompt_body, seq=seq, top_k=top_k, n_span=n_span),
        out_shape=jax.ShapeDtypeStruct((m, hd), BF16),
        grid=(n_seq, nq),
        in_specs=[row_spec,
                  pl.BlockSpec((None, N_IDX_HEADS, QBLOCK, IDX_DIM), lambda b, j: (b, 0, j, 0)),
                  pl.BlockSpec((None, N_IDX_HEADS, QBLOCK), lambda b, j: (b, 0, j)),
                  pl.BlockSpec((None, seq, IDX_DIM), lambda b, j: (b, 0, 0)),
                  pl.BlockSpec((None, seq, kvw), lambda b, j: (b, 0, 0)),
                  pl.BlockSpec((None, kvw, seq), lambda b, j: (b, 0, 0)),
                  row_spec],
        out_specs=row_spec,
        scratch_shapes=[pltpu.VMEM((seq, QBLOCK), F32), pltpu.VMEM((seq, QBLOCK), F32)],
        compiler_params=_cparams("parallel", "arbitrary"),
        name="dsa_prompt",
    )(q, qi, wt, kidx, kb, vt, zc)


def _page_copy(cache_hbm, layer, page, dst, sem):
    return pltpu.make_async_copy(cache_hbm.at[layer, page], dst, sem)


def _dsa_sample_select_body(pt_ref, qi_ref, wcol_ref, knew_ref, ckidx_hbm, bias_ref, buf, sem, sc_scr, scn_scr,
                            *, layer, group, n_pages, top_k):
    step = pl.program_id(0)
    n_steps = pl.num_programs(0)
    past = n_pages * PAGE_SIZE

    def copies(step_idx, slot):
        out = []
        for i in range(group):
            for p in range(n_pages):
                page = pt_ref[step_idx * group + i, p]
                dst = buf.at[slot, i, :, pl.ds(p * PAGE_SIZE, PAGE_SIZE)]
                out.append(_page_copy(ckidx_hbm, layer, page, dst, sem.at[slot]))
        return out

    @pl.when(step == 0)
    def _():
        for c in copies(0, 0):
            c.start()

    @pl.when(step + 1 < n_steps)
    def _():
        for c in copies(step + 1, (step + 1) % 2):
            c.start()

    slot = step % 2
    for c in copies(step, slot):
        c.wait()

    rows, news = [], []
    for i in range(group):
        qi = qi_ref[i]
        wcol = wcol_ref[i]
        keys_t = buf[slot, i].astype(BF16)
        logit = jnp.dot(qi, keys_t, preferred_element_type=F32)
        rows.append(jnp.sum(jnp.maximum(logit, 0.0) * wcol, axis=0, keepdims=True))
        knew = knew_ref[i].astype(BF16).astype(F32)
        lnew = jnp.sum(qi.astype(F32) * knew, axis=1, keepdims=True)
        news.append(jnp.sum(jnp.maximum(lnew, 0.0) * wcol, axis=0, keepdims=True))
    r0 = pl.multiple_of(step * group, group)
    sc_scr[pl.ds(r0, group), :] = jnp.concatenate(rows, axis=0)
    scn_scr[pl.ds(r0, group), :] = jnp.broadcast_to(jnp.concatenate(news, axis=0), (group, LANES))

    @pl.when(step == n_steps - 1)
    def _():
        sc = sc_scr[...]
        sc_new = scn_scr[...]
        first = lax.broadcasted_iota(jnp.int32, sc_new.shape, 1) == 0
        k = float(top_k)

        def count(pred, pred_new):
            return (jnp.sum(jnp.where(pred, 1.0, 0.0), axis=1, keepdims=True)
                    + jnp.sum(jnp.where(first, jnp.where(pred_new, 1.0, 0.0), 0.0), axis=1, keepdims=True))

        thr = _kth_largest(lambda c: count(sc >= c, sc_new >= c), k, (sc.shape[0], 1))
        bias_ref[:, 0:past] = jnp.where(sc >= thr, 0.0, -jnp.inf)
        bias_ref[:, past:past + LANES] = jnp.where(first, jnp.where(sc_new >= thr, 0.0, -jnp.inf), -jnp.inf)

        @pl.when(_any_true(count(sc >= thr, sc_new >= thr) > k))
        def _():
            above = jnp.where(sc > thr, 1.0, 0.0)
            tied = jnp.where(sc == thr, 1.0, 0.0)
            need = k - count(sc > thr, sc_new > thr)
            tied_b = tied.astype(BF16)
            cols = math.gcd(past, 2 * LANES)
            for c0 in range(0, past, cols):
                r = lax.broadcasted_iota(jnp.int32, (past, cols), 0)
                c = c0 + lax.broadcasted_iota(jnp.int32, (past, cols), 1)
                earlier = jnp.where(r < c, 1.0, 0.0).astype(BF16)
                rank = jnp.dot(tied_b, earlier, preferred_element_type=F32)
                keep = above[:, c0:c0 + cols] + tied[:, c0:c0 + cols] * jnp.where(rank < need, 1.0, 0.0)
                bias_ref[:, c0:c0 + cols] = jnp.where(keep > 0.5, 0.0, -jnp.inf)
            rank_new = jnp.sum(tied, axis=1, keepdims=True)
            keep_new = jnp.where(sc_new > thr, 1.0, 0.0) + jnp.where(sc_new == thr, 1.0, 0.0) * jnp.where(
                rank_new < need, 1.0, 0.0)
            bias_ref[:, past:past + LANES] = jnp.where(first, jnp.where(keep_new > 0.5, 0.0, -jnp.inf), -jnp.inf)


def _dsa_sample_select(page_table, qi, wcol, knew, cache_kidx_t, *, layer, group=8):
    n_seq, n_pages = page_table.shape
    past = n_pages * PAGE_SIZE
    top_k = min(TOPK_MAX, (past + 1) // 4)
    assert n_seq % group == 0
    grid_spec = pltpu.PrefetchScalarGridSpec(
        num_scalar_prefetch=1,
        grid=(n_seq // group,),
        in_specs=[pl.BlockSpec((group, N_IDX_HEADS, IDX_DIM), lambda s, pt: (s, 0, 0)),
                  pl.BlockSpec((group, N_IDX_HEADS, 1), lambda s, pt: (s, 0, 0)),
                  pl.BlockSpec((group, 1, IDX_DIM), lambda s, pt: (s, 0, 0)),
                  pl.BlockSpec(memory_space=pl.ANY)],
        out_specs=pl.BlockSpec((n_seq, past + LANES), lambda s, pt: (0, 0)),
        scratch_shapes=[pltpu.VMEM((2, group, IDX_DIM, past), F32),
                        pltpu.SemaphoreType.DMA((2,)),
                        pltpu.VMEM((n_seq, past), F32),
                        pltpu.VMEM((n_seq, LANES), F32)],
    )
    return pl.pallas_call(
        functools.partial(_dsa_sample_select_body, layer=layer, group=group, n_pages=n_pages, top_k=top_k),
        out_shape=jax.ShapeDtypeStruct((n_seq, past + LANES), F32),
        grid_spec=grid_spec,
        compiler_params=_cparams("arbitrary"),
        name="dsa_sample_select",
    )(page_table, qi, wcol, knew, cache_kidx_t)


def _dsa_sample_attend_body(pt_ref, q_ref, bias_ref, knew_ref, vnew_ref, zc_ref, ck_hbm, cv_hbm, o_ref,
                            kbuf, vbuf, sem, *, layer, n_pages):
    b = pl.program_id(0)
    n_seq = pl.num_programs(0)
    past = n_pages * PAGE_SIZE
    page_rows = PAGE_SIZE * N_KV

    def copies(seq_idx, slot):
        out = []
        for p in range(n_pages):
            page = pt_ref[seq_idx, p]
            rows = pl.ds(p * page_rows, page_rows)
            out.append(_page_copy(ck_hbm, layer, page, kbuf.at[slot, rows, :], sem.at[0, slot]))
            out.append(_page_copy(cv_hbm, layer, page, vbuf.at[slot, rows, :], sem.at[1, slot]))
        return out

    @pl.when(b == 0)
    def _():
        for c in copies(0, 0):
            c.start()

    @pl.when(b + 1 < n_seq)
    def _():
        for c in copies(b + 1, (b + 1) % 2):
            c.start()

    slot = b % 2
    for c in copies(b, slot):
        c.wait()

    bias = bias_ref[:, 0:past]
    bias_new = bias_ref[:, past:past + 1]
    outs = []
    for n in range(N_KV):
        qn = q_ref[n]
        kn = kbuf[slot, pl.ds(n, past, stride=N_KV), :].astype(BF16)
        vn = vbuf[slot, pl.ds(n, past, stride=N_KV), :].astype(BF16)
        lanes = slice(n * HEAD_DIM, (n + 1) * HEAD_DIM)
        knew = knew_ref[:, lanes].astype(BF16).astype(F32)
        vnew = vnew_ref[:, lanes].astype(BF16).astype(F32)
        s = lax.dot_general(qn, kn, (((1,), (1,)), ((), ())), preferred_element_type=F32) + bias
        s_new = jnp.sum(qn.astype(F32) * knew, axis=1, keepdims=True) + bias_new
        mx = jnp.maximum(jnp.max(s, axis=1, keepdims=True), s_new)
        p = jnp.exp(s - mx)
        p_new = jnp.exp(s_new - mx)
        den = jnp.sum(p, axis=1, keepdims=True) + p_new
        o = jnp.dot(p.astype(BF16), vn, preferred_element_type=F32) + p_new.astype(BF16).astype(F32) * vnew
        outs.append(o / den)
    o_ref[...] = (jnp.concatenate(outs, axis=0) * zc_ref[...].astype(F32)).astype(o_ref.dtype)


def _dsa_sample_attend(page_table, q, bias, knew, vnew, zc, cache_k, cache_v, *, layer):
    n_seq, n_pages = page_table.shape
    past = n_pages * PAGE_SIZE
    kvw = N_KV * HEAD_DIM
    group = N_HEADS // N_KV
    grid_spec = pltpu.PrefetchScalarGridSpec(
        num_scalar_prefetch=1,
        grid=(n_seq,),
        in_specs=[pl.BlockSpec((None, N_KV, group, HEAD_DIM), lambda b, pt: (b, 0, 0, 0)),
                  pl.BlockSpec((None, 1, past + LANES), lambda b, pt: (b, 0, 0)),
                  pl.BlockSpec((None, 1, kvw), lambda b, pt: (b, 0, 0)),
                  pl.BlockSpec((None, 1, kvw), lambda b, pt: (b, 0, 0)),
                  pl.BlockSpec((None, N_HEADS, HEAD_DIM), lambda b, pt: (b, 0, 0)),
                  pl.BlockSpec(memory_space=pl.ANY),
                  pl.BlockSpec(memory_space=pl.ANY)],
        out_specs=pl.BlockSpec((None, N_HEADS, HEAD_DIM), lambda b, pt: (b, 0, 0)),
        scratch_shapes=[pltpu.VMEM((2, past * N_KV, HEAD_DIM), F32), pltpu.VMEM((2, past * N_KV, HEAD_DIM), F32),
                        pltpu.SemaphoreType.DMA((2, 2))],
    )
    return pl.pallas_call(
        functools.partial(_dsa_sample_attend_body, layer=layer, n_pages=n_pages),
        out_shape=jax.ShapeDtypeStruct((n_seq, N_HEADS, HEAD_DIM), BF16),
        grid_spec=grid_spec,
        compiler_params=_cparams("arbitrary"),
        name="dsa_sample_attend",
    )(page_table, q, bias, knew, vnew, zc, cache_k, cache_v)


def _merge_body(a_ref, b_ref, c_ref, wa_ref, wb_ref, wc_ref, ga_ref, gb_ref, gc_ref, o_ref):
    acc = ga_ref[...].astype(F32) * jnp.dot(a_ref[...], wa_ref[...], preferred_element_type=F32)
    acc = acc + gb_ref[...].astype(F32) * jnp.dot(b_ref[...], wb_ref[...], preferred_element_type=F32)
    acc = acc + gc_ref[...].astype(F32) * jnp.dot(c_ref[...], wc_ref[...], preferred_element_type=F32)
    o_ref[...] = acc.astype(o_ref.dtype)


def _merge(a_in, b_in, c_in, w_pa, w_pb, w_pc, gm, *, tm, tn=512):
    m, k = a_in.shape
    d = w_pa.shape[1]
    tm, tn = min(tm, m), min(tn, d)
    nj = d // tn
    x_spec = pl.BlockSpec((tm, k), lambda i, j: (i, 0))
    w_spec = pl.BlockSpec((k, tn), lambda i, j: (0, j))
    g_specs = [pl.BlockSpec((tm, tn), lambda i, j, br=br: (i, br * nj + j)) for br in range(N_BRANCH)]
    return pl.pallas_call(
        _merge_body,
        out_shape=jax.ShapeDtypeStruct((m, d), BF16),
        grid=(m // tm, nj),
        in_specs=[x_spec, x_spec, x_spec, w_spec, w_spec, w_spec] + g_specs,
        out_specs=pl.BlockSpec((tm, tn), lambda i, j: (i, j)),
        compiler_params=_cparams("parallel", "arbitrary"),
        name="merge",
    )(a_in, b_in, c_in, w_pa, w_pb, w_pc, gm, gm, gm)


def _out_body(mg_ref, w_ref, x_ref, g_ref, gate_ref, o_ref):
    o = jnp.dot(mg_ref[...], w_ref[...], preferred_element_type=F32)
    y = o * lax.rsqrt(jnp.mean(o * o, axis=-1, keepdims=True) + EPS) * g_ref[...]
    o_ref[...] = x_ref[...] + gate_ref[...] * y


def _out_proj(merged, w_out, x, g_post, mod, *, per_row, rows_per_seq, tm):
    m, d = x.shape
    tm = min(tm, m)
    row_spec = pl.BlockSpec((tm, d), lambda i: (i, 0))
    return pl.pallas_call(
        _out_body,
        out_shape=jax.ShapeDtypeStruct((m, d), F32),
        grid=(m // tm,),
        in_specs=[row_spec, pl.BlockSpec((d, d), lambda i: (0, 0)), row_spec,
                  pl.BlockSpec((1, d), lambda i: (0, 0)),
                  _mod_spec(per_row, 2, tm, d, rows_per_seq)],
        out_specs=row_spec,
        compiler_params=_cparams("parallel"),
        name="out_proj",
    )(merged, w_out, x, g_post, mod)


def _split_points(d, w_a, w_b):
    sizes = (w_a, w_a, w_a, w_b, w_b, N_HEADS * HEAD_DIM, N_KV * HEAD_DIM, N_KV * HEAD_DIM,
             N_IDX_HEADS * IDX_DIM, IDX_DIM, N_IDX_HEADS, N_HEADS * HEAD_DIM, N_BRANCH * d)
    offs = [0]
    for s in sizes:
        offs.append(offs[-1] + s)
    return offs


def _project_all(h, w_in, offs, tm):
    def seg(i, **kw):
        return _matmul(h, w_in[:, offs[i]:offs[i + 1]].astype(BF16), tm=tm, tn=512, **kw)

    out = {
        "u": seg(0, out_dtype=BF16, act="gelu"),
        "vg": seg(1, out_dtype=F32, act="gelu"),
        "za": seg(2, out_dtype=BF16, act="silu"),
        "xb": seg(3, out_dtype=F32),
        "zb": seg(4, out_dtype=BF16, act="silu"),
        "q": seg(5, out_dtype=BF16, scale=HEAD_DIM ** -0.5),
        "k": seg(6, out_dtype=F32),
        "v": seg(7, out_dtype=F32),
        "qi": seg(8, out_dtype=BF16, scale=IDX_DIM ** -0.5),
        "zc": seg(11, out_dtype=BF16, act="silu"),
        "gm": seg(12, out_dtype=BF16, act="sigmoid"),
    }
    w_small = w_in[:, offs[9]:offs[11]].astype(BF16)
    w_small = jnp.pad(w_small, ((0, 0), (0, LANES - w_small.shape[1])))
    small = _matmul(h, w_small, out_dtype=F32, tm=tm, tn=LANES)
    out["kidx"] = small[:, :IDX_DIM]
    out["widx"] = small[:, IDX_DIM:IDX_DIM + N_IDX_HEADS] * (N_IDX_HEADS ** -0.5)
    return out


def _layer_weights(l, w):
    d = w["w_in"].shape[1]
    w_a = w["g_v"].shape[1]
    return {
        "g_pre": w["g_pre"][l][None], "g_v": w["g_v"][l][None], "g_post": w["g_post"][l][None],
        "w_in": w["w_in"][l], "w_s": w["w_s"][l], "b_s_t": w["b_s"][l].T,
        "ws0": jnp.repeat(w["w_s"][l][:, 0, 0], w_a // N_A_GROUPS)[None],
        "bs0": jnp.repeat(w["b_s"][l][:, 0], w_a // N_A_GROUPS)[None],
        "w_conv": w["w_conv"][l], "b_conv": w["b_conv"][l][None],
        "w_rg_a": w["w_rg_a"][l].astype(BF16), "b_rg_a": w["b_rg_a"][l][None],
        "w_rg_x": w["w_rg_x"][l].astype(BF16), "b_rg_x": w["b_rg_x"][l][None],
        "lam": w["lam"][l][None],
        "w_pa": w["w_pa"][l].astype(BF16), "w_pb": w["w_pb"][l].astype(BF16),
        "w_pc": w["w_pc"][l].astype(BF16), "w_out": w["w_out"][l].astype(BF16),
        "d": d,
    }


def _prompt_layer(x, mod, lw, offs, *, n_seq, seq):
    tm = min(1024, seq)
    common = dict(per_row=False, rows_per_seq=seq)
    h = _prenorm(x, lw["g_pre"], mod, tm=min(512, seq), **common)
    pr = _project_all(h, lw["w_in"], offs, tm)
    a_in = _gmlp_prompt(pr["u"], pr["vg"], pr["za"], lw["g_v"], lw["w_s"], lw["b_s_t"])
    b_in, h_t = _rglru_prompt(pr["xb"], pr["zb"], lw["w_conv"], lw["b_conv"], lw["w_rg_a"], lw["b_rg_a"],
                              lw["w_rg_x"], lw["b_rg_x"], lw["lam"], n_seq=n_seq, seq=seq)
    kvw = N_KV * HEAD_DIM
    qi = pr["qi"].reshape(n_seq, seq, N_IDX_HEADS, IDX_DIM).transpose(0, 2, 1, 3)
    wt = pr["widx"].reshape(n_seq, seq, N_IDX_HEADS).transpose(0, 2, 1)
    kidx_b = pr["kidx"].astype(BF16).reshape(n_seq, seq, IDX_DIM)
    kb = pr["k"].astype(BF16).reshape(n_seq, seq, kvw)
    vt = pr["v"].astype(BF16).reshape(n_seq, seq, kvw).transpose(0, 2, 1)
    c_in = _dsa_prompt(pr["q"], qi, wt, kidx_b, kb, vt, pr["zc"], n_seq=n_seq, seq=seq)
    merged = _merge(a_in, b_in, c_in, lw["w_pa"], lw["w_pb"], lw["w_pc"], pr["gm"], tm=min(512, seq))
    y = _out_proj(merged, lw["w_out"], x, lw["g_post"], mod, tm=min(512, seq), **common)
    w_b = pr["xb"].shape[1]
    state = (pr["k"].reshape(n_seq, seq, N_KV, HEAD_DIM), pr["v"].reshape(n_seq, seq, N_KV, HEAD_DIM),
             pr["kidx"].reshape(n_seq, seq, IDX_DIM),
             pr["xb"].reshape(n_seq, seq, w_b)[:, seq - (CONV_W - 1):],
             h_t.reshape(n_seq, w_b))
    return y, state


def _sample_layer(x, mod, lw, offs, layer, conv_state, h0, page_table, cache_k, cache_v, cache_kidx_t):
    n_seq = x.shape[0]
    common = dict(per_row=True, rows_per_seq=1)
    h = _prenorm(x, lw["g_pre"], mod, tm=n_seq, **common)
    pr = _project_all(h, lw["w_in"], offs, n_seq)
    a_in, v_a = _gmlp_sample(pr["u"], pr["vg"], pr["za"], lw["g_v"], lw["ws0"], lw["bs0"])
    w_b = pr["xb"].shape[1]
    b_in, h_new = _rglru_sample(pr["xb"], pr["zb"], jnp.swapaxes(conv_state, 0, 1), h0,
                                lw["w_conv"], lw["b_conv"], lw["w_rg_a"], lw["b_rg_a"], lw["w_rg_x"],
                                lw["b_rg_x"], lw["lam"])
    kvw = N_KV * HEAD_DIM
    qi = pr["qi"].reshape(n_seq, N_IDX_HEADS, IDX_DIM)
    wcol = pr["widx"].reshape(n_seq, N_IDX_HEADS, 1)
    bias = _dsa_sample_select(page_table, qi, wcol, pr["kidx"].reshape(n_seq, 1, IDX_DIM), cache_kidx_t,
                              layer=layer)
    q4 = pr["q"].reshape(n_seq, N_KV, N_HEADS // N_KV, HEAD_DIM)
    c_in = _dsa_sample_attend(page_table, q4, bias.reshape(n_seq, 1, -1), pr["k"].reshape(n_seq, 1, kvw),
                              pr["v"].reshape(n_seq, 1, kvw), pr["zc"].reshape(n_seq, N_HEADS, HEAD_DIM),
                              cache_k, cache_v, layer=layer)
    merged = _merge(a_in, b_in, c_in.reshape(n_seq, N_HEADS * HEAD_DIM), lw["w_pa"], lw["w_pb"], lw["w_pc"],
                    pr["gm"], tm=n_seq)
    y = _out_proj(merged, lw["w_out"], x, lw["g_post"], mod, tm=n_seq, **common)
    new_conv = jnp.concatenate([conv_state[:, 1:], pr["xb"][:, None, :]], axis=1)
    state = (pr["k"].reshape(n_seq, 1, N_KV, HEAD_DIM), pr["v"].reshape(n_seq, 1, N_KV, HEAD_DIM),
             pr["kidx"].reshape(n_seq, 1, IDX_DIM), new_conv, h_new, v_a.reshape(n_seq, 1, -1))
    return y, state


def kernel(x_prompt, x_sample, c_prompt, c_sample, cache_k, cache_v, cache_kidx, state_conv, state_h, page_table, w_mod, b_mod, g_pre, w_in, g_v, w_s, b_s, w_conv, b_conv, w_rg_a, b_rg_a, w_rg_x, b_rg_x, lam, w_pa, w_pb, w_pc, w_out, g_post):
    n_seq, seq, d = x_prompt.shape
    n_dec, dec_seq, _ = x_sample.shape
    assert dec_seq == 1 and seq % QBLOCK == 0
    depth = w_in.shape[0]
    w_a, w_b = g_v.shape[1], lam.shape[1]
    offs = _split_points(d, w_a, w_b)
    weights = dict(w_mod=w_mod, b_mod=b_mod, g_pre=g_pre, w_in=w_in, g_v=g_v, w_s=w_s, b_s=b_s, w_conv=w_conv,
                   b_conv=b_conv, w_rg_a=w_rg_a, b_rg_a=b_rg_a, w_rg_x=w_rg_x, b_rg_x=b_rg_x, lam=lam,
                   w_pa=w_pa, w_pb=w_pb, w_pc=w_pc, w_out=w_out, g_post=g_post)
    n_phys = cache_k.shape[1]
    ck = cache_k.reshape(depth, n_phys, PAGE_SIZE * N_KV, HEAD_DIM)
    cv = cache_v.reshape(depth, n_phys, PAGE_SIZE * N_KV, HEAD_DIM)
    ckidx_t = jnp.swapaxes(cache_kidx, 2, 3)

    n_c = n_seq + n_dec
    pad = -n_c % SUBLANES
    c_all = jnp.concatenate([c_prompt, c_sample, jnp.zeros((pad, d), F32)], axis=0)

    y_p = x_prompt.reshape(n_seq * seq, d)
    y_s = x_sample.reshape(n_dec, d)
    st_p, st_s = [], []
    for l in range(depth):
        lw = _layer_weights(l, weights)
        mod = _modulation(c_all, w_mod[l].astype(BF16), b_mod[l][None])
        mod_p = mod[:n_seq].reshape(n_seq, 1, 3 * d)
        mod_s = mod[n_seq:n_c]
        y_p, sp = _prompt_layer(y_p, mod_p, lw, offs, n_seq=n_seq, seq=seq)
        y_s, ss = _sample_layer(y_s, mod_s, lw, offs, l, state_conv[l], state_h[l], page_table, ck, cv,
                                ckidx_t)
        st_p.append(sp)
        st_s.append(ss)
    outs_p = [jnp.stack([s[i] for s in st_p]) for i in range(5)]
    outs_s = [jnp.stack([s[i] for s in st_s]) for i in range(6)]
    return (y_p.reshape(n_seq, seq, d), y_s.reshape(n_dec, 1, d), *outs_p, *outs_s)
```
